```python
import math
import jax, jax.numpy as jnp
from jax import lax
import numpy as np

D_MODEL = 1024
BATCH = 4
SEQ = 8192
DEPTH = 1
DEC_BATCH = 128
DEC_SEQ = 4
PAST_LEN = 8192
PAGE_SIZE = 128

HEAD_DIM = 64
N_HEADS_FOX = 8
N_HEADS_DSA = 8
W_FOX = N_HEADS_FOX * HEAD_DIM
W_DSA = N_HEADS_DSA * HEAD_DIM
N_IDX_HEADS = 8
IDX_DIM = 64
IDX_W_SCALE = (N_IDX_HEADS * IDX_DIM) ** -0.5
TOPK_MAX = 256
N_BUCKETS = 32
MAX_DISTANCE = 128
Q_BLOCK = 128
ATTN_SCALE = HEAD_DIM ** -0.5
RMS_EPS = 1e-6
FORGET_BIAS_INIT = 3.0
D_IN = 4 * W_FOX + N_HEADS_FOX + 4 * W_DSA + N_IDX_HEADS * IDX_DIM + IDX_DIM + N_IDX_HEADS + 2 * D_MODEL

kernel_name = 'hybrid_fox_dsa_decoder_step'


def rms_norm(x, g):
    x32 = x.astype(jnp.float32)
    y = x32 * lax.rsqrt(jnp.mean(x32 * x32, axis=-1, keepdims=True) + RMS_EPS)
    return (y * g.astype(jnp.float32)).astype(x.dtype)


def t5_bucket(rel):
    max_exact = N_BUCKETS // 2
    n = jnp.maximum(rel, 0)
    nf = jnp.maximum(n, 1).astype(jnp.float32)
    large = max_exact + (jnp.log(nf / max_exact) / math.log(MAX_DISTANCE / max_exact) * (N_BUCKETS - max_exact)).astype(jnp.int32)
    return jnp.where(n < max_exact, n, jnp.minimum(large, N_BUCKETS - 1))


def split_projection(proj):
    sizes = [W_FOX, W_FOX, W_FOX, N_HEADS_FOX, W_FOX,
             W_DSA, W_DSA, W_DSA, N_IDX_HEADS * IDX_DIM, IDX_DIM, N_IDX_HEADS, W_DSA,
             D_MODEL, D_MODEL]
    return jnp.split(proj, np.cumsum(sizes)[:-1].tolist(), axis=-1)


def mixer_inputs(x, c, g_pre, w_mod, b_mod, w_in, b_forget):
    bsz, t, _ = x.shape
    mod = jnp.einsum('bd,de->be', jax.nn.silu(c), w_mod) + b_mod
    shift, scale, gate = jnp.split(mod, 3, axis=-1)
    h = rms_norm(x, g_pre) * (1 + scale[:, None, :]) + shift[:, None, :]
    proj = jnp.einsum('btd,de->bte', h, w_in)
    fq, fk, fv, ff, fz, dq, dk, dv, iq, ik, iw, dz, gf, gd = split_projection(proj)
    heads = lambda a, n: a.reshape(bsz, t, n, -1)
    return dict(
        fox_q=heads(fq, N_HEADS_FOX), fox_k=heads(fk, N_HEADS_FOX), fox_v=heads(fv, N_HEADS_FOX),
        fox_logf=jax.nn.log_sigmoid((ff + b_forget).astype(jnp.float32)), fox_z=fz,
        dsa_q=heads(dq, N_HEADS_DSA), dsa_k=heads(dk, N_HEADS_DSA), dsa_v=heads(dv, N_HEADS_DSA),
        idx_q=heads(iq, N_IDX_HEADS), idx_k=ik, idx_w=iw.astype(jnp.float32) * IDX_W_SCALE, dsa_z=dz,
        gate_fox=gf, gate_dsa=gd, gate=gate)


def mixer_output(x, o_fox, o_dsa, m, w_fox_out, w_dsa_out, w_out, g_post):
    bsz, t, _ = x.shape
    a = jnp.einsum('bte,ed->btd', o_fox.reshape(bsz, t, W_FOX) * jax.nn.silu(m['fox_z']), w_fox_out)
    b = jnp.einsum('bte,ed->btd', o_dsa.reshape(bsz, t, W_DSA) * jax.nn.silu(m['dsa_z']), w_dsa_out)
    merged = jax.nn.sigmoid(m['gate_fox']) * a + jax.nn.sigmoid(m['gate_dsa']) * b
    out = jnp.einsum('btd,de->bte', merged, w_out)
    return x + m['gate'][:, None, :] * rms_norm(out, g_post)


def index_scores(q_idx, k_idx, w_idx):
    r = jax.nn.relu(jnp.einsum('bqhd,bsd->bqhs', q_idx, k_idx, preferred_element_type=jnp.float32))
    return jnp.einsum('bqhs,bqh->bqs', r, w_idx)


def sparse_attend(q, k_sel, v_sel, qpos, sel, rel_bias):
    rel = qpos[None, :, None] - sel
    s = jnp.einsum('bqhd,bqkhd->bhqk', q, k_sel, preferred_element_type=jnp.float32) * ATTN_SCALE
    s = s + rel_bias[t5_bucket(rel)].astype(jnp.float32).transpose(0, 3, 1, 2)
    s = jnp.where((rel >= 0)[:, None], s, -jnp.inf)
    p = jax.nn.softmax(s, axis=-1)
    return jnp.einsum('bhqk,bqkhd->bqhd', p.astype(v_sel.dtype), v_sel)


def fox_prompt(q, k, v, logf):
    bsz, seq = q.shape[:2]
    f_cum = lax.cumsum(logf, axis=1).transpose(0, 2, 1)
    kpos = jnp.arange(seq)

    def block(i):
        q0 = i * Q_BLOCK
        qb = lax.dynamic_slice_in_dim(q, q0, Q_BLOCK, axis=1)
        fb = lax.dynamic_slice_in_dim(f_cum, q0, Q_BLOCK, axis=2)
        qpos = q0 + jnp.arange(Q_BLOCK)
        s = jnp.einsum('bqhd,bkhd->bhqk', qb, k, preferred_element_type=jnp.float32) * ATTN_SCALE
        s = s + fb[..., :, None] - f_cum[..., None, :]
        s = jnp.where(kpos[None, :] <= qpos[:, None], s, -jnp.inf)
        p = jax.nn.softmax(s, axis=-1)
        return jnp.einsum('bhqk,bkhd->bqhd', p.astype(v.dtype), v)

    out = lax.map(block, jnp.arange(seq // Q_BLOCK))
    return out.transpose(1, 0, 2, 3, 4).reshape(bsz, seq, N_HEADS_FOX, HEAD_DIM)


def fox_sample(q, k_new, v_new, logf_new, cache_k, cache_v, cache_logf, page_table, layer):
    dbsz, t = q.shape[:2]
    n_pages = page_table.shape[1]
    lp = cache_logf[layer, page_table].astype(jnp.float32).reshape(dbsz, n_pages * PAGE_SIZE, N_HEADS_FOX)
    suffix = lax.cumsum(lp, axis=1, reverse=True) - lp
    suffix_pages = suffix.reshape(dbsz, n_pages, PAGE_SIZE, N_HEADS_FOX).transpose(1, 0, 3, 2)
    cn = lax.cumsum(logf_new, axis=1).transpose(0, 2, 1)
    qpos = jnp.arange(t)
    s_new = jnp.einsum('bqhd,bkhd->bhqk', q, k_new, preferred_element_type=jnp.float32) * ATTN_SCALE
    s_new = s_new + cn[..., :, None] - cn[..., None, :]
    s_new = jnp.where(qpos[None, :] <= qpos[:, None], s_new, -jnp.inf)
    m0 = jnp.max(s_new, axis=-1)
    p0 = jnp.exp(s_new - m0[..., None])
    carry0 = (m0, jnp.sum(p0, axis=-1), jnp.einsum('bhqk,bkhd->bhqd', p0, v_new.astype(jnp.float32)))

    def page_step(carry, xs):
        m, l, acc = carry
        pages, suf = xs
        kp = cache_k[layer, pages]
        vp = cache_v[layer, pages]
        s = jnp.einsum('bqhd,bkhd->bhqk', q, kp, preferred_element_type=jnp.float32) * ATTN_SCALE
        s = s + cn[..., :, None] + suf[:, :, None, :]
        m_new = jnp.maximum(m, jnp.max(s, axis=-1))
        alpha = jnp.exp(m - m_new)
        p = jnp.exp(s - m_new[..., None])
        acc = acc * alpha[..., None] + jnp.einsum('bhqk,bkhd->bhqd', p, vp.astype(jnp.float32))
        return (m_new, l * alpha + jnp.sum(p, axis=-1), acc), None

    (m, l, acc), _ = lax.scan(page_step, carry0, (page_table.T, suffix_pages))
    return (acc / l[..., None]).transpose(0, 2, 1, 3).astype(v_new.dtype)


def dsa_prompt(q, k, v, q_idx, k_idx, w_idx, rel_bias):
    bsz, seq = q.shape[:2]
    topk = min(TOPK_MAX, seq // 4)
    kpos = jnp.arange(seq)
    bidx = jnp.arange(bsz)[:, None, None]

    def block(i):
        q0 = i * Q_BLOCK
        qpos = q0 + jnp.arange(Q_BLOCK)
        sl = lambda a: lax.dynamic_slice_in_dim(a, q0, Q_BLOCK, axis=1)
        scores = index_scores(sl(q_idx), k_idx, sl(w_idx))
        scores = jnp.where(kpos[None, :] <= qpos[:, None], scores, -jnp.inf)
        _, sel = lax.top_k(scores, topk)
        return sparse_attend(sl(q), k[bidx, sel], v[bidx, sel], qpos, sel, rel_bias)

    out = lax.map(block, jnp.arange(seq // Q_BLOCK))
    return out.transpose(1, 0, 2, 3, 4).reshape(bsz, seq, N_HEADS_DSA, HEAD_DIM)


def dsa_sample(q, k_new, v_new, q_idx, k_idx_new, w_idx, cache_k, cache_v, cache_idx_k, page_table, rel_bias, layer):
    dbsz, t = q.shape[:2]
    n_pages = page_table.shape[1]
    past = n_pages * PAGE_SIZE
    topk = min(TOPK_MAX, (past + t) // 4)
    k_idx_past = cache_idx_k[layer, page_table].reshape(dbsz, past, IDX_DIM)
    k_idx_all = jnp.concatenate([k_idx_past, k_idx_new.astype(k_idx_past.dtype)], axis=1)
    qpos = past + jnp.arange(t)
    kpos = jnp.arange(past + t)
    scores = index_scores(q_idx, k_idx_all, w_idx)
    scores = jnp.where(kpos[None, :] <= qpos[:, None], scores, -jnp.inf)
    _, sel = lax.top_k(scores, topk)
    bidx = jnp.arange(dbsz)[:, None, None]
    in_past = (sel < past)[..., None, None]
    sp = jnp.minimum(sel, past - 1)
    phys_page = page_table[bidx, sp // PAGE_SIZE]
    offset = sp % PAGE_SIZE
    sn = jnp.clip(sel - past, 0, t - 1)
    k_sel = jnp.where(in_past, cache_k[layer, phys_page, offset], k_new[bidx, sn])
    v_sel = jnp.where(in_past, cache_v[layer, phys_page, offset], v_new[bidx, sn])
    return sparse_attend(q, k_sel, v_sel, qpos, sel, rel_bias)


def setup_inputs(seed: int = 0) -> dict:
    key = jax.random.key(seed)
    ks = jax.random.split(key, 24)
    n_pages = PAST_LEN // PAGE_SIZE
    n_used = DEC_BATCH * n_pages
    n_phys = n_used + max(1, n_used // 4)
    nrm = lambda k, shape, s=1.0: s * jax.random.normal(k, shape, jnp.float32)
    page_table = jax.random.permutation(ks[0], n_phys)[:n_used].reshape(DEC_BATCH, n_pages).astype(jnp.int32)
    return {
        'x_prompt': nrm(ks[1], (BATCH, SEQ, D_MODEL)),
        'x_sample': nrm(ks[2], (DEC_BATCH, DEC_SEQ, D_MODEL)),
        'cache_fox_k': nrm(ks[3], (DEPTH, n_phys, PAGE_SIZE, N_HEADS_FOX, HEAD_DIM)),
        'cache_fox_v': nrm(ks[4], (DEPTH, n_phys, PAGE_SIZE, N_HEADS_FOX, HEAD_DIM)),
        'cache_fox_logf': jax.nn.log_sigmoid(FORGET_BIAS_INIT + nrm(ks[5], (DEPTH, n_phys, PAGE_SIZE, N_HEADS_FOX))),
        'cache_dsa_k': nrm(ks[6], (DEPTH, n_phys, PAGE_SIZE, N_HEADS_DSA, HEAD_DIM)),
        'cache_dsa_v': nrm(ks[7], (DEPTH, n_phys, PAGE_SIZE, N_HEADS_DSA, HEAD_DIM)),
        'cache_idx_k': nrm(ks[8], (DEPTH, n_phys, PAGE_SIZE, IDX_DIM)),
        'page_table': page_table,
        'c_prompt': nrm(ks[9], (BATCH, D_MODEL)),
        'c_sample': nrm(ks[10], (DEC_BATCH, D_MODEL)),
        'g_pre': 1.0 + nrm(ks[11], (DEPTH, D_MODEL), 0.05),
        'w_mod': nrm(ks[12], (DEPTH, D_MODEL, 3 * D_MODEL), 0.5 * D_MODEL ** -0.5),
        'b_mod': nrm(ks[13], (DEPTH, 3 * D_MODEL), 0.02),
        'w_in': nrm(ks[14], (DEPTH, D_MODEL, D_IN), D_MODEL ** -0.5),
        'b_forget': FORGET_BIAS_INIT + nrm(ks[15], (DEPTH, N_HEADS_FOX), 0.1),
        'rel_bias': nrm(ks[16], (N_BUCKETS, N_HEADS_DSA), 0.5),
        'w_fox_out': nrm(ks[17], (DEPTH, W_FOX, D_MODEL), W_FOX ** -0.5),
        'w_dsa_out': nrm(ks[18], (DEPTH, W_DSA, D_MODEL), W_DSA ** -0.5),
        'w_out': nrm(ks[19], (DEPTH, D_MODEL, D_MODEL), D_MODEL ** -0.5),
        'g_post': 1.0 + nrm(ks[20], (DEPTH, D_MODEL), 0.05),
    }


def reference(x_prompt, x_sample, cache_fox_k, cache_fox_v, cache_fox_logf, cache_dsa_k, cache_dsa_v, cache_idx_k,
              page_table, c_prompt, c_sample, g_pre, w_mod, b_mod, w_in, b_forget, rel_bias,
              w_fox_out, w_dsa_out, w_out, g_post):
    pk, pv, pf, pdk, pdv, pik = [], [], [], [], [], []
    sk, sv, sf, sdk, sdv, sik = [], [], [], [], [], []
    h_p, h_s = x_prompt, x_sample
    for l in range(DEPTH):
        fw = (g_pre[l], w_mod[l], b_mod[l], w_in[l], b_forget[l])
        bw = (w_fox_out[l], w_dsa_out[l], w_out[l], g_post[l])
        mp = mixer_inputs(h_p, c_prompt, *fw)
        o_fox = fox_prompt(mp['fox_q'], mp['fox_k'], mp['fox_v'], mp['fox_logf'])
        o_dsa = dsa_prompt(mp['dsa_q'], mp['dsa_k'], mp['dsa_v'], mp['idx_q'], mp['idx_k'], mp['idx_w'], rel_bias)
        h_p = mixer_output(h_p, o_fox, o_dsa, mp, *bw)
        ms = mixer_inputs(h_s, c_sample, *fw)
        o_fox = fox_sample(ms['fox_q'], ms['fox_k'], ms['fox_v'], ms['fox_logf'],
                           cache_fox_k, cache_fox_v, cache_fox_logf, page_table, l)
        o_dsa = dsa_sample(ms['dsa_q'], ms['dsa_k'], ms['dsa_v'], ms['idx_q'], ms['idx_k'], ms['idx_w'],
                           cache_dsa_k, cache_dsa_v, cache_idx_k, page_table, rel_bias, l)
        h_s = mixer_output(h_s, o_fox, o_dsa, ms, *bw)
        pk.append(mp['fox_k']); pv.append(mp['fox_v']); pf.append(mp['fox_logf'])
        pdk.append(mp['dsa_k']); pdv.append(mp['dsa_v']); pik.append(mp['idx_k'])
        sk.append(ms['fox_k']); sv.append(ms['fox_v']); sf.append(ms['fox_logf'])
        sdk.append(ms['dsa_k']); sdv.append(ms['dsa_v']); sik.append(ms['idx_k'])
    return (h_p, h_s,
            jnp.stack(pk), jnp.stack(pv), jnp.stack(pf), jnp.stack(pdk), jnp.stack(pdv), jnp.stack(pik),
            jnp.stack(sk), jnp.stack(sv), jnp.stack(sf), jnp.stack(sdk), jnp.stack(sdv), jnp.stack(sik))
```

```python
import functools

import numpy as np
import jax
import jax.numpy as jnp
from jax import lax
from jax.experimental import pallas as pl
from jax.experimental.pallas import tpu as pltpu

F32 = jnp.float32
BF16 = jnp.bfloat16
I32 = jnp.int32

LANES = 128
TOPK_MAX = 256
MAX_DISTANCE = 128
RMS_EPS = 1e-6
NEG_INF = float("-inf")
M_INIT = -1e30
INT_MIN = -(2 ** 31)
INT_MAX = 2 ** 31 - 1
N_AUG = 3
VMEM_LIMIT = 60 * 1024 * 1024

NT = (((1,), (1,)), ((), ()))


def _cparams(sem):
    return pltpu.CompilerParams(dimension_semantics=sem, vmem_limit_bytes=VMEM_LIMIT)


def _resident(block_shape, index_map):
    return pl.BlockSpec(block_shape, index_map, pipeline_mode=pl.Buffered(1))


def _split3(x):
    hi = x.astype(BF16)
    r1 = x - hi.astype(F32)
    mid = r1.astype(BF16)
    lo = (r1 - mid.astype(F32)).astype(BF16)
    return hi, mid, lo


def _dot(a, b):
    return jnp.dot(a, b, preferred_element_type=F32)


def _dot_nt(a, b):
    return lax.dot_general(a, b, NT, preferred_element_type=F32)


def _sigmoid(x):
    return 1.0 / (1.0 + jnp.exp(-x))


def _silu(x):
    return x * _sigmoid(x)


def _mod_kernel(c_ref, w_ref, b_ref, o_ref):
    a = _silu(c_ref[...])
    a_hi, a_mid, _ = _split3(a)
    w_hi, w_mid, _ = _split3(w_ref[...])
    o_ref[...] = _dot(a_hi, w_hi) + (_dot(a_hi, w_mid) + _dot(a_mid, w_hi)) + b_ref[...]


def _mod(c, w_mod, b_mod):
    n, d = c.shape
    e = w_mod.shape[1]
    te = 512
    assert e % te == 0 and n % 8 == 0
    return pl.pallas_call(
        _mod_kernel,
        grid=(e // te,),
        in_specs=[pl.BlockSpec((n, d), lambda j: (0, 0)),
                  pl.BlockSpec((d, te), lambda j: (0, j)),
                  pl.BlockSpec((1, te), lambda j: (0, j))],
        out_specs=pl.BlockSpec((n, te), lambda j: (0, j)),
        out_shape=jax.ShapeDtypeStruct((n, e), F32),
        compiler_params=_cparams(("arbitrary",)),
        name="mod",
    )(c, w_mod, b_mod.reshape(1, e))


def _proj_kernel(x_ref, sc_ref, sh_ref, g_ref, w_ref, bf_ref, pl_ref, *refs,
                 cols, prompt, group, tm, nh, dh, nhi, di, idx_w_scale):
    if prompt:
        (fk_o, fv_o, lf_o, dk_o, dv_o, ik_o, fz_o, dz_o, gf_o, gd_o, iw_o,
         qa_o, ka_o, fvh_o, dqh_o, dkh_o, dvh_o, iqh_o, ikb_o, carry) = refs
    else:
        (fk_o, fv_o, lf_o, dk_o, dv_o, ik_o, fz_o, dz_o, gf_o, gd_o, iw_o,
         fq_o, dq_o, iq_o, cn_o, carry) = refs
    i = pl.program_id(0)
    x = x_ref[...]
    y = x * lax.rsqrt(jnp.mean(x * x, axis=-1, keepdims=True) + RMS_EPS) * g_ref[...]
    hb = (y * (1.0 + sc_ref[...]) + sh_ref[...]).astype(BF16)

    def mm(name):
        c0, c1 = cols[name]
        return _dot(hb, w_ref[:, c0:c1])

    fk = mm("fk"); fk_o[...] = fk
    fv = mm("fv"); fv_o[...] = fv
    dk = mm("dk"); dk_o[...] = dk
    dv = mm("dv"); dv_o[...] = dv
    fz_o[...] = mm("fz")
    dz_o[...] = mm("dz")
    gf_o[...] = mm("gf")
    gd_o[...] = mm("gd")
    small = mm("small")
    ik = small[:, LANES - di:]
    ik_o[...] = ik
    iw_o[...] = small[:, 8:8 + nhi] * idx_w_scale
    z = small + bf_ref[...]
    logf_full = jnp.minimum(z, 0.0) - jnp.log(1.0 + jnp.exp(-jnp.abs(z)))
    lane = lax.broadcasted_iota(I32, (tm, LANES), 1)
    logf = jnp.where(lane < nh, logf_full, 0.0)
    lf_o[...] = logf[:, :nh]

    r = lax.broadcasted_iota(I32, (tm, tm), 0)
    c = lax.broadcasted_iota(I32, (tm, tm), 1)
    g_in = min(group, tm)
    tri = jnp.where((c <= r) & ((c // g_in) == (r // g_in)), 1.0, 0.0).astype(BF16)
    l_hi, l_mid, l_lo = _split3(logf)
    cum = _dot(tri, l_lo) + _dot(tri, l_mid) + _dot(tri, l_hi)
    if group > tm:
        @pl.when((i * tm) % group == 0)
        def _():
            carry[...] = jnp.zeros_like(carry)
        cum = cum + carry[...]
        carry[...] = cum[tm - 1:tm, :]

    if prompt:
        c_hi, c_mid, c_lo = _split3(cum)
        aug = _dot(c_hi, pl_ref[0]) + _dot(c_mid, pl_ref[1]) + _dot(c_lo, pl_ref[2]) + pl_ref[3][0:1, :].astype(F32)
        wa = nh * LANES
        qa = mm("qa") + aug[:, :wa]
        ka = mm("ka") + aug[:, wa:]
        dq = mm("dq"); iq = mm("iq")
        for h in range(nh):
            qa_o[h] = qa[:, h * LANES:(h + 1) * LANES].astype(BF16)
            ka_o[h] = ka[:, h * LANES:(h + 1) * LANES].astype(BF16)
            fvh_o[h] = fv[:, h * dh:(h + 1) * dh].astype(BF16)
            dqh_o[h] = dq[:, h * dh:(h + 1) * dh].astype(BF16)
            dkh_o[h] = dk[:, h * dh:(h + 1) * dh].astype(BF16)
            dvh_o[h] = dv[:, h * dh:(h + 1) * dh].astype(BF16)
        for h in range(nhi):
            iqh_o[h] = iq[:, h * di:(h + 1) * di].astype(BF16)
        ikb_o[...] = ik.astype(BF16)
    else:
        fq_o[...] = mm("fq").astype(BF16)
        dq_o[...] = mm("dq").astype(BF16)
        iq_o[...] = mm("iq").astype(BF16)
        cn_o[...] = cum


def _proj_weights(w_in, b_forget, prompt, nh, dh, nhi, di, d, attn_scale):
    wf = nh * dh
    sizes = [wf, wf, wf, nh, wf, wf, wf, wf, nhi * di, di, nhi, wf, d, d]
    offs = np.cumsum([0] + sizes)
    fq, fk, fv, ff, fz, dq, dk, dv, iq, ik, iw, dz, gf, gd = [w_in[:, offs[j]:offs[j + 1]] for j in range(14)]
    fq = fq * attn_scale
    dq = dq * attn_scale
    small = jnp.zeros((d, LANES), F32)
    small = small.at[:, 0:nh].set(ff).at[:, 8:8 + nhi].set(iw).at[:, LANES - di:].set(ik)

    def pad_heads(w):
        return jnp.pad(w.reshape(d, nh, dh), ((0, 0), (0, 0), (0, LANES - dh))).reshape(d, nh * LANES)

    if prompt:
        groups = [("qa", pad_heads(fq)), ("ka", pad_heads(fk))]
    else:
        groups = [("fq", fq)]
    groups += [("fk", fk), ("fv", fv), ("fz", fz), ("dq", dq), ("dk", dk), ("dv", dv), ("iq", iq), ("dz", dz),
               ("gf", gf), ("gd", gd), ("small", small)]
    cols, c0 = {}, 0
    for name, w in groups:
        cols[name] = (c0, c0 + w.shape[1])
        c0 += w.shape[1]
    w_all = jnp.concatenate([w for _, w in groups], axis=1).astype(BF16)
    bf_row = jnp.zeros((1, LANES), F32).at[0, 0:nh].set(b_forget)
    return w_all, cols, bf_row


def _placement(nh, dh):
    wa = nh * LANES
    p = np.zeros((N_AUG + 1, LANES, 2 * wa), np.float32)
    for h in range(nh):
        for c in range(N_AUG):
            p[c, h, h * LANES + dh + c] = 1.0
            p[c, h, wa + h * LANES + dh + N_AUG + c] = -1.0
            p[N_AUG, :, h * LANES + dh + N_AUG + c] = 1.0
            p[N_AUG, :, wa + h * LANES + dh + c] = 1.0
    return jnp.asarray(p, BF16)


def _proj(x2, scale, shift, g_pre, w_in, b_forget, *, prompt, bsz, seq, nh, dh, nhi, di, attn_scale, idx_w_scale):
    n, d = x2.shape
    wf = nh * dh
    assert dh + 2 * N_AUG <= LANES and nh <= 8 and nhi <= 8 and di <= LANES - 16
    w_all, cols, bf_row = _proj_weights(w_in, b_forget, prompt, nh, dh, nhi, di, d, attn_scale)
    place = _placement(nh, dh)
    tm = 256 if prompt else min(n, 256)
    assert n % tm == 0 and seq % min(seq, tm) == 0
    nt = seq // tm if prompt else 1
    row = lambda i: (i, 0)
    if prompt:
        mod_spec = pl.BlockSpec((None, 1, d), lambda i: (i // nt, 0, 0))
        hm = lambda i: (i // nt, 0, i % nt, 0)
    else:
        mod_spec = pl.BlockSpec((tm, d), row)
    f32o = lambda w: (jax.ShapeDtypeStruct((n, w), F32), pl.BlockSpec((tm, w), row))
    outs = [f32o(wf), f32o(wf), f32o(nh), f32o(wf), f32o(wf), f32o(di), f32o(wf), f32o(wf), f32o(d), f32o(d), f32o(nhi)]
    if prompt:
        hm_o = lambda heads, w: (jax.ShapeDtypeStruct((bsz, heads, seq, w), BF16), pl.BlockSpec((None, heads, tm, w), hm))
        outs += [hm_o(nh, LANES), hm_o(nh, LANES), hm_o(nh, dh), hm_o(nh, dh), hm_o(nh, dh), hm_o(nh, dh), hm_o(nhi, di),
                 (jax.ShapeDtypeStruct((bsz, seq, di), BF16), pl.BlockSpec((None, tm, di), lambda i: (i // nt, i % nt, 0)))]
    else:
        b16o = lambda w: (jax.ShapeDtypeStruct((n, w), BF16), pl.BlockSpec((tm, w), row))
        outs += [b16o(wf), b16o(wf), b16o(nhi * di), f32o(LANES)]
    kern = functools.partial(_proj_kernel, cols=cols, prompt=prompt, group=seq, tm=tm, nh=nh, dh=dh, nhi=nhi, di=di,
                             idx_w_scale=idx_w_scale)
    return pl.pallas_call(
        kern,
        grid=(n // tm,),
        in_specs=[pl.BlockSpec((tm, d), row), mod_spec, mod_spec,
                  _resident((1, d), lambda i: (0, 0)),
                  _resident(w_all.shape, lambda i: (0, 0)),
                  _resident((1, LANES), lambda i: (0, 0)),
                  _resident(place.shape, lambda i: (0, 0, 0))],
        out_specs=[o[1] for o in outs],
        out_shape=[o[0] for o in outs],
        scratch_shapes=[pltpu.VMEM((1, LANES), F32)],
        compiler_params=_cparams(("arbitrary",)),
        name="proj_prompt" if prompt else "proj_sample",
    )(x2, scale, shift, g_pre.reshape(1, d), w_all, bf_row, place)


def _attend(carry, s, v):
    m, l, acc = carry
    m_new = jnp.maximum(m, jnp.max(s, axis=1, keepdims=True))
    alpha = jnp.exp(m - m_new)
    p = jnp.exp(s - m_new)
    l = alpha * l + jnp.sum(p, axis=1, keepdims=True)
    acc = alpha * acc + _dot(p.astype(BF16), v)
    return m_new, l, acc


def _attend_init(rows, dv):
    return (jnp.full((rows, 1), M_INIT, F32), jnp.zeros((rows, 1), F32), jnp.zeros((rows, dv), F32))


def _fox_kernel(q_ref, k_ref, v_ref, o_ref, *, tq):
    qi = pl.program_id(2)
    q = q_ref[...]
    dv = v_ref.shape[-1]

    def block(kb):
        k = k_ref[pl.ds(pl.multiple_of(kb * tq, tq), tq), :]
        v = v_ref[pl.ds(pl.multiple_of(kb * tq, tq), tq), :]
        return _dot_nt(q, k), v

    def full(kb, carry):
        s, v = block(kb)
        return _attend(carry, s, v)

    carry = lax.fori_loop(0, qi, full, _attend_init(tq, dv))
    s, v = block(qi)
    r = lax.broadcasted_iota(I32, (tq, tq), 0)
    c = lax.broadcasted_iota(I32, (tq, tq), 1)
    _, l, acc = _attend(carry, jnp.where(c <= r, s, NEG_INF), v)
    o_ref[...] = acc / l


def _fox_prompt(qa, ka, fv):
    bsz, nh, seq, wa = qa.shape
    dv = fv.shape[-1]
    tq = min(256, seq)
    assert seq % tq == 0
    return pl.pallas_call(
        functools.partial(_fox_kernel, tq=tq),
        grid=(bsz, nh, seq // tq),
        in_specs=[pl.BlockSpec((None, None, tq, wa), lambda b, h, i: (b, h, i, 0)),
                  pl.BlockSpec((None, None, seq, wa), lambda b, h, i: (b, h, 0, 0)),
                  pl.BlockSpec((None, None, seq, dv), lambda b, h, i: (b, h, 0, 0))],
        out_specs=pl.BlockSpec((None, None, tq, dv), lambda b, h, i: (b, h, i, 0)),
        out_shape=jax.ShapeDtypeStruct((bsz, nh, seq, dv), F32),
        compiler_params=_cparams(("arbitrary", "arbitrary", "arbitrary")),
        name="fox_prompt",
    )(qa, ka, fv)


def _sort_key(x):
    bits = lax.bitcast_convert_type(x, I32)
    return jnp.where(bits < 0, bits ^ jnp.int32(INT_MAX), bits)


def _select_topk(sc_ref, nslab4, rows, kk, idx_bits):
    lane = lax.broadcasted_iota(I32, (rows, LANES), 1)
    kf = jnp.float32(kk)

    def count(pred):
        def body(c4, cnt):
            for j in range(4):
                s = c4 * 4 + j
                cnt = cnt + jnp.where(pred(sc_ref[s], s), 1.0, 0.0)
            return cnt
        cnt = lax.fori_loop(0, nslab4, body, jnp.zeros((rows, LANES), F32))
        return jnp.sum(cnt, axis=1, keepdims=True)

    def bit_step(i, thr):
        cand = thr + jnp.left_shift(jnp.int32(1), 31 - i)
        c = count(lambda k, s: k >= cand)
        return jnp.where(c >= kf, cand, thr)

    thr = lax.fori_loop(0, 32, bit_step, jnp.full((rows, 1), INT_MIN, I32))
    thr = jnp.maximum(thr, INT_MIN + 1)
    c_gt = count(lambda k, s: k > thr)
    c_ge = count(lambda k, s: k >= thr)
    need = kf - c_gt

    def idx_step(i, cut):
        cand = cut + jnp.left_shift(jnp.int32(1), idx_bits - 1 - i)
        c = count(lambda k, s: (k == thr) & ((s * LANES + lane) < cand))
        return jnp.where(c < need, cand, cut)

    cut = lax.fori_loop(0, idx_bits, idx_step, jnp.zeros((rows, 1), I32))
    cut = jnp.where(c_ge > kf, cut, INT_MAX)

    def fin(c4, _):
        for j in range(4):
            s = c4 * 4 + j
            k = sc_ref[s]
            sel = (k > thr) | ((k == thr) & ((s * LANES + lane) <= cut))
            sc_ref[s] = lax.bitcast_convert_type(jnp.where(sel, 0.0, NEG_INF), I32)
        return 0

    lax.fori_loop(0, nslab4, fin, 0)


def _mask_slab(sc_ref, s):
    return lax.bitcast_convert_type(sc_ref[s], F32)


def _dsa_kernel(iq_ref, iw_ref, ik_ref, dq_ref, dk_ref, dv_ref, tab_ref, o_ref, sc_ref, *, tq, topk, nh, nhi, idx_bits):
    qi = pl.program_id(1)
    nch = (qi + 4) // 4
    cw = 4 * LANES
    w = iw_ref[...]
    row = qi * tq + lax.broadcasted_iota(I32, (tq, cw), 0)
    dv = dv_ref.shape[-1]

    def scores(c, _):
        k = ik_ref[pl.ds(pl.multiple_of(c * cw, cw), cw), :]
        acc = jnp.zeros((tq, cw), F32)
        for h in range(nhi):
            acc = acc + jnp.maximum(_dot_nt(iq_ref[h], k), 0.0) * w[:, h:h + 1]
        col = c * cw + lax.broadcasted_iota(I32, (tq, cw), 1)
        key = jnp.where(col <= row, _sort_key(acc), INT_MIN)
        for j in range(4):
            sc_ref[c * 4 + j] = key[:, j * LANES:(j + 1) * LANES]
        return 0

    lax.fori_loop(0, nch, scores, 0)
    _select_topk(sc_ref, nch, tq, topk, idx_bits)

    nfar = jnp.maximum(qi - 1, 0) // 4

    def head(h, _):
        q = dq_ref[h]

        def far(c, carry):
            k = dk_ref[h, pl.ds(pl.multiple_of(c * cw, cw), cw), :]
            v = dv_ref[h, pl.ds(pl.multiple_of(c * cw, cw), cw), :]
            mask = jnp.concatenate([_mask_slab(sc_ref, c * 4 + j) for j in range(4)], axis=1)
            return _attend(carry, _dot_nt(q, k) + mask, v)

        def near(kb, carry):
            k = dk_ref[h, pl.ds(pl.multiple_of(kb * LANES, LANES), LANES), :]
            v = dv_ref[h, pl.ds(pl.multiple_of(kb * LANES, LANES), LANES), :]
            s = _dot_nt(q, k) + _mask_slab(sc_ref, kb) + tab_ref[jnp.minimum(qi - kb, 2), h]
            return _attend(carry, s, v)

        carry = lax.fori_loop(0, nfar, far, _attend_init(tq, dv))
        _, l, acc = lax.fori_loop(nfar * 4, qi + 1, near, carry)
        o_ref[h] = acc / l
        return 0

    lax.fori_loop(0, nh, head, 0)


def _t5_bucket(rel, n_buckets):
    max_exact = n_buckets // 2
    n = jnp.maximum(rel, 0)
    nf = jnp.maximum(n, 1).astype(F32)
    large = max_exact + (jnp.log(nf / max_exact) / np.log(MAX_DISTANCE / max_exact) * (n_buckets - max_exact)).astype(I32)
    return jnp.where(n < max_exact, n, jnp.minimum(large, n_buckets - 1))


def _dsa_prompt(iq_h, iw, ik_b, dq_h, dk_h, dv_h, rel_bias):
    bsz, nh, seq, dh = dq_h.shape
    nhi, di = iq_h.shape[1], iq_h.shape[3]
    n_buckets = rel_bias.shape[0]
    tq = LANES
    assert seq % (4 * LANES) == 0 and tq == MAX_DISTANCE
    topk = min(TOPK_MAX, seq // 4)
    i = jnp.arange(tq)
    rel = jnp.arange(2)[:, None, None] * tq + i[None, :, None] - i[None, None, :]
    far = rel_bias[n_buckets - 1].astype(F32)
    tab = rel_bias[_t5_bucket(rel, n_buckets)].astype(F32) - far
    tab = jnp.concatenate([tab.transpose(0, 3, 1, 2), jnp.zeros((1, nh, tq, tq), F32)], axis=0)
    nslab = seq // LANES
    kern = functools.partial(_dsa_kernel, tq=tq, topk=topk, nh=nh, nhi=nhi, idx_bits=max(1, (seq - 1).bit_length()))
    return pl.pallas_call(
        kern,
        grid=(bsz, seq // tq),
        in_specs=[pl.BlockSpec((None, nhi, tq, di), lambda b, i: (b, 0, i, 0)),
                  pl.BlockSpec((None, tq, nhi), lambda b, i: (b, i, 0)),
                  _resident((None, seq, di), lambda b, i: (b, 0, 0)),
                  pl.BlockSpec((None, nh, tq, dh), lambda b, i: (b, 0, i, 0)),
                  _resident((None, nh, seq, dh), lambda b, i: (b, 0, 0, 0)),
                  _resident((None, nh, seq, dh), lambda b, i: (b, 0, 0, 0)),
                  _resident(tab.shape, lambda b, i: (0, 0, 0, 0))],
        out_specs=pl.BlockSpec((None, nh, tq, dh), lambda b, i: (b, 0, i, 0)),
        out_shape=jax.ShapeDtypeStruct((bsz, nh, seq, dh), F32),
        scratch_shapes=[pltpu.VMEM((nslab, tq, LANES), I32)],
        compiler_params=_cparams(("arbitrary", "arbitrary")),
        name="dsa_prompt",
    )(iq_h, iw.reshape(bsz, seq, nhi), ik_b, dq_h, dk_h, dv_h, tab)


def _out_kernel(x_ref, gate_ref, of_ref, fz_ref, od_ref, dz_ref, gf_ref, gd_ref, wf_ref, wd_ref, wo_ref, gp_ref, y_ref,
                *, head_major, nh, dh):
    def branch(o_ref, z_ref, w_ref):
        g = _silu(z_ref[...])
        if head_major:
            a = None
            for h in range(nh):
                t = _dot((o_ref[h] * g[:, h * dh:(h + 1) * dh]).astype(BF16), w_ref[h * dh:(h + 1) * dh, :])
                a = t if a is None else a + t
            return a
        return _dot((o_ref[...] * g).astype(BF16), w_ref[...])

    a = branch(of_ref, fz_ref, wf_ref)
    b = branch(od_ref, dz_ref, wd_ref)
    merged = _sigmoid(gf_ref[...]) * a + _sigmoid(gd_ref[...]) * b
    out = _dot(merged.astype(BF16), wo_ref[...])
    nrm = out * lax.rsqrt(jnp.mean(out * out, axis=-1, keepdims=True) + RMS_EPS) * gp_ref[...]
    y_ref[...] = x_ref[...] + gate_ref[...] * nrm


def _mix_out(x2, gate, o_fox, fz, o_dsa, dz, gf, gd, w_fox_out, w_dsa_out, w_out, g_post, *, head_major, seq, nh, dh):
    n, d = x2.shape
    wf = nh * dh
    tm = min(256, n)
    assert n % tm == 0
    row = lambda i: (i, 0)
    if head_major:
        nt = seq // tm
        o_spec = pl.BlockSpec((None, nh, tm, dh), lambda i: (i // nt, 0, i % nt, 0))
        gate_spec = pl.BlockSpec((None, 1, d), lambda i: (i // nt, 0, 0))
    else:
        o_spec = pl.BlockSpec((tm, wf), row)
        gate_spec = pl.BlockSpec((tm, d), row)
    return pl.pallas_call(
        functools.partial(_out_kernel, head_major=head_major, nh=nh, dh=dh),
        grid=(n // tm,),
        in_specs=[pl.BlockSpec((tm, d), row), gate_spec, o_spec, pl.BlockSpec((tm, wf), row),
                  o_spec, pl.BlockSpec((tm, wf), row), pl.BlockSpec((tm, d), row), pl.BlockSpec((tm, d), row),
                  _resident((wf, d), lambda i: (0, 0)), _resident((wf, d), lambda i: (0, 0)),
                  _resident((d, d), lambda i: (0, 0)), _resident((1, d), lambda i: (0, 0))],
        out_specs=pl.BlockSpec((tm, d), row),
        out_shape=jax.ShapeDtypeStruct((n, d), F32),
        compiler_params=_cparams(("arbitrary",)),
        name="mix_out_prompt" if head_major else "mix_out_sample",
    )(x2, gate, o_fox, fz, o_dsa, dz, gf, gd, w_fox_out.astype(BF16), w_dsa_out.astype(BF16), w_out.astype(BF16),
      g_post.reshape(1, d))


def _sidx_kernel(pt_ref, q_ref, w_ref, kn_ref, *refs, n_pages, t_new, past):
    pages = refs[:n_pages]
    o_ref = refs[n_pages]
    q = q_ref[...]
    w = w_ref[...]

    def slab(kt):
        r = jnp.maximum(_dot(q, kt.astype(BF16)), 0.0) * w
        return jnp.concatenate([jnp.sum(r[8 * t:8 * t + 8], axis=0, keepdims=True) for t in range(t_new)], axis=0)

    for j in range(n_pages):
        o_ref[:, j * LANES:(j + 1) * LANES] = slab(pages[j][...])
    new = slab(kn_ref[...])
    lane = lax.broadcasted_iota(I32, (t_new, LANES), 1)
    trow = lax.broadcasted_iota(I32, (t_new, LANES), 0)
    o_ref[:, past:past + LANES] = jnp.where(lane <= trow, new, NEG_INF)
    pad = o_ref.shape[-1] - past - LANES
    if pad:
        o_ref[:, past + LANES:] = jnp.full((t_new, pad), NEG_INF, F32)


def _sample_idx_scores(page_table, q32, w32, ik_new_pad, cache_idx_k, layer, lpad):
    dbsz, n_pages = page_table.shape
    t_new = q32.shape[1] // 8
    di = q32.shape[2]
    page = cache_idx_k.shape[3]
    assert page == LANES
    page_specs = [pl.BlockSpec((None, None, di, page), (lambda b, pt, j=j: (layer, pt[b, j], 0, 0))) for j in range(n_pages)]
    return pl.pallas_call(
        functools.partial(_sidx_kernel, n_pages=n_pages, t_new=t_new, past=n_pages * page),
        grid_spec=pltpu.PrefetchScalarGridSpec(
            num_scalar_prefetch=1,
            grid=(dbsz,),
            in_specs=[pl.BlockSpec((None, t_new * 8, di), lambda b, pt: (b, 0, 0)),
                      pl.BlockSpec((None, t_new * 8, LANES), lambda b, pt: (b, 0, 0)),
                      pl.BlockSpec((None, di, page), lambda b, pt: (b, 0, 0))] + page_specs,
            out_specs=pl.BlockSpec((None, t_new, lpad), lambda b, pt: (b, 0, 0)),
        ),
        out_shape=jax.ShapeDtypeStruct((dbsz, t_new, lpad), F32),
        compiler_params=_cparams(("arbitrary",)),
        name="sample_idx_scores",
    )(page_table, q32, w32, ik_new_pad, *([cache_idx_k] * n_pages))


def _ssel_kernel(s_ref, o_ref, sc_ref, *, rows, t_new, past, topk, idx_bits):
    nslab = sc_ref.shape[0]
    lane = lax.broadcasted_iota(I32, (rows, LANES), 1)
    t = lax.broadcasted_iota(I32, (rows, LANES), 0) % t_new
    for s in range(nslab):
        valid = (s * LANES + lane) <= past + t
        sc_ref[s] = jnp.where(valid, _sort_key(s_ref[:, s * LANES:(s + 1) * LANES]), INT_MIN)
    _select_topk(sc_ref, nslab // 4, rows, topk, idx_bits)
    for s in range(nslab):
        o_ref[:, s * LANES:(s + 1) * LANES] = _mask_slab(sc_ref, s)


def _sample_select(scores2, t_new, past, topk):
    n, lpad = scores2.shape
    rows = min(n, LANES)
    assert n % rows == 0 and rows % t_new == 0 and lpad % (4 * LANES) == 0
    nslab = lpad // LANES
    return pl.pallas_call(
        functools.partial(_ssel_kernel, rows=rows, t_new=t_new, past=past, topk=topk, idx_bits=max(1, (lpad - 1).bit_length())),
        grid=(n // rows,),
        in_specs=[pl.BlockSpec((rows, lpad), lambda i: (i, 0))],
        out_specs=pl.BlockSpec((rows, lpad), lambda i: (i, 0)),
        out_shape=jax.ShapeDtypeStruct((n, lpad), F32),
        scratch_shapes=[pltpu.VMEM((nslab, rows, LANES), I32)],
        compiler_params=_cparams(("arbitrary",)),
        name="sample_select",
    )(scores2)


def _wt_kernel(lp_ref, o_ref):
    pb = lp_ref.shape[0]
    lp = lp_ref[...].reshape(pb * 8, LANES)
    r = lax.broadcasted_iota(I32, (LANES, 2 * LANES), 0)
    c = lax.broadcasted_iota(I32, (LANES, 2 * LANES), 1)
    u = jnp.where((c >= LANES) | (r > c), 1.0, 0.0).astype(BF16)
    hi, mid, lo = _split3(lp)
    res = _dot(lo, u) + _dot(mid, u) + _dot(hi, u)
    o_ref[:, 0:8, :] = res[:, :LANES].reshape(pb, 8, LANES)
    o_ref[:, 8:16, :] = res[:, LANES:].reshape(pb, 8, LANES)


def _page_suffix(logf_t):
    n_phys = logf_t.shape[0]
    pb = 256
    while n_phys % pb:
        pb //= 2
    return pl.pallas_call(
        _wt_kernel,
        grid=(n_phys // pb,),
        in_specs=[pl.BlockSpec((pb, 8, LANES), lambda i: (i, 0, 0))],
        out_specs=pl.BlockSpec((pb, 16, LANES), lambda i: (i, 0, 0)),
        out_shape=jax.ShapeDtypeStruct((n_phys, 16, LANES), F32),
        compiler_params=_cparams(("arbitrary",)),
        name="page_suffix",
    )(logf_t)


def _sattn_kernel(pt_ref, qf_ref, qd_ref, knf_ref, vnf_ref, knd_ref, vnd_ref, cn_ref, bnf_ref, selb_ref, seln_ref,
                  t5_ref, t5n_ref, *refs, ppc, nc, t_new, nh, dh):
    kf_p, vf_p, kd_p, vd_p, wt_p = (refs[i * ppc:(i + 1) * ppc] for i in range(5))
    of_ref, od_ref = refs[5 * ppc:5 * ppc + 2]
    qbf, qbd, mf, lf, af, md, ld, ad, car = refs[5 * ppc + 2:]
    c = pl.program_id(1)
    rows = t_new * 8
    wf = nh * dh
    headmask = (lax.broadcasted_iota(I32, (8, wf), 1) // dh) == lax.broadcasted_iota(I32, (8, wf), 0)

    def expand_rows(x):
        return jnp.concatenate([jnp.broadcast_to(x[t:t + 1], (8, x.shape[1])) for t in range(t_new)], axis=0)

    def step(state, q, ks, bias, vs):
        m_ref, l_ref, a_ref = state
        s = jnp.concatenate([_dot(q, k.astype(BF16)) for k in ks], axis=1) + bias
        m_old = m_ref[...]
        m_new = jnp.maximum(m_old, jnp.max(s, axis=1, keepdims=True))
        alpha = jnp.exp(m_old - m_new)
        p = jnp.exp(s - m_new)
        pb = p.astype(BF16)
        pv = None
        for j, v in enumerate(vs):
            t = _dot_nt(pb[:, j * LANES:(j + 1) * LANES], v.astype(BF16))
            pv = t if pv is None else pv + t
        m_ref[...] = m_new
        l_ref[...] = alpha * l_ref[...] + jnp.sum(p, axis=1, keepdims=True)
        a_ref[...] = alpha * a_ref[...] + pv

    @pl.when(c == 0)
    def _():
        hm = jnp.where(headmask, 1.0, 0.0)
        qbf[...] = (expand_rows(qf_ref[...].astype(F32)) * jnp.concatenate([hm] * t_new, axis=0)).astype(BF16)
        qbd[...] = (expand_rows(qd_ref[...].astype(F32)) * jnp.concatenate([hm] * t_new, axis=0)).astype(BF16)
        for m_ref, l_ref, a_ref in ((mf, lf, af), (md, ld, ad)):
            m_ref[...] = jnp.full(m_ref.shape, M_INIT, F32)
            l_ref[...] = jnp.zeros(l_ref.shape, F32)
            a_ref[...] = jnp.zeros(a_ref.shape, F32)
        car[...] = jnp.zeros(car.shape, F32)
        step((mf, lf, af), qbf[...], [knf_ref[...]], bnf_ref[...], [vnf_ref[...]])
        step((md, ld, ad), qbd[...], [knd_ref[...]], expand_rows(seln_ref[...]) + t5n_ref[...], [vnd_ref[...]])

    carry = car[...]
    cn = cn_ref[...]
    bias = [None] * ppc
    for j in reversed(range(ppc)):
        wt = wt_p[j][...]
        suf = wt[0:8] + carry
        bias[j] = jnp.concatenate([suf] * t_new, axis=0) + cn
        carry = carry + wt[8:16]
    car[...] = carry
    step((mf, lf, af), qbf[...], [r[...] for r in kf_p], jnp.concatenate(bias, axis=1), [r[...] for r in vf_p])
    step((md, ld, ad), qbd[...], [r[...] for r in kd_p], expand_rows(selb_ref[...]) + t5_ref[nc - 1 - c],
         [r[...] for r in vd_p])

    @pl.when(c == nc - 1)
    def _():
        hm = jnp.concatenate([jnp.where(headmask, 1.0, 0.0)] * t_new, axis=0)
        for (l_ref, a_ref), o_ref in (((lf, af), of_ref), ((ld, ad), od_ref)):
            o = (a_ref[...] / l_ref[...]) * hm
            o_ref[...] = jnp.concatenate([jnp.sum(o[8 * t:8 * t + 8], axis=0, keepdims=True) for t in range(t_new)], axis=0)


def _sample_attention(page_table, layer, qf, qd, knf, vnf, knd, vnd, cn32, bnf, selb, seln, t5c, t5n,
                      cache_fox_k, cache_fox_v, cache_dsa_k, cache_dsa_v, wt, *, nh, dh):
    dbsz, n_pages = page_table.shape
    t_new = qf.shape[1]
    wf = nh * dh
    page = cache_fox_k.shape[3]
    ppc = 8
    while n_pages % ppc:
        ppc //= 2
    nc = n_pages // ppc
    rows = t_new * 8
    cwid = ppc * page
    assert page == LANES and nh == 8

    def pg(j):
        return lambda b, c, pt: (layer, pt[b, (nc - 1 - c) * ppc + j], 0, 0)

    def wtm(j):
        return lambda b, c, pt: (pt[b, (nc - 1 - c) * ppc + j], 0, 0)

    kv_specs = [pl.BlockSpec((None, None, wf, page), pg(j)) for j in range(ppc)]
    wt_specs = [pl.BlockSpec((None, 16, LANES), wtm(j)) for j in range(ppc)]
    per_b3 = lambda shp: pl.BlockSpec((None,) + shp, lambda b, c, pt: (b, 0, 0))
    caches = lambda a: [a] * ppc
    return pl.pallas_call(
        functools.partial(_sattn_kernel, ppc=ppc, nc=nc, t_new=t_new, nh=nh, dh=dh),
        grid_spec=pltpu.PrefetchScalarGridSpec(
            num_scalar_prefetch=1,
            grid=(dbsz, nc),
            in_specs=[per_b3((t_new, wf)), per_b3((t_new, wf)),
                      per_b3((wf, page)), per_b3((wf, page)), per_b3((wf, page)), per_b3((wf, page)),
                      per_b3((rows, LANES)), per_b3((rows, LANES)),
                      pl.BlockSpec((None, None, t_new, cwid), lambda b, c, pt: (b, nc - 1 - c, 0, 0)),
                      per_b3((t_new, LANES)),
                      _resident(t5c.shape, lambda b, c, pt: (0, 0, 0)),
                      _resident(t5n.shape, lambda b, c, pt: (0, 0))]
                     + kv_specs * 4 + wt_specs,
            out_specs=[per_b3((t_new, wf)), per_b3((t_new, wf))],
            scratch_shapes=[pltpu.VMEM((rows, wf), BF16), pltpu.VMEM((rows, wf), BF16),
                            pltpu.VMEM((rows, 1), F32), pltpu.VMEM((rows, 1), F32), pltpu.VMEM((rows, wf), F32),
                            pltpu.VMEM((rows, 1), F32), pltpu.VMEM((rows, 1), F32), pltpu.VMEM((rows, wf), F32),
                            pltpu.VMEM((8, LANES), F32)],
        ),
        out_shape=[jax.ShapeDtypeStruct((dbsz, t_new, wf), F32)] * 2,
        compiler_params=_cparams(("arbitrary", "arbitrary")),
        name="sample_attention",
    )(page_table, qf, qd, knf, vnf, knd, vnd, cn32, bnf, selb, seln, t5c, t5n,
      *caches(cache_fox_k), *caches(cache_fox_v), *caches(cache_dsa_k), *caches(cache_dsa_v), *([wt] * ppc))


def kernel(x_prompt, x_sample, cache_fox_k, cache_fox_v, cache_fox_logf, cache_dsa_k, cache_dsa_v, cache_idx_k,
           page_table, c_prompt, c_sample, g_pre, w_mod, b_mod, w_in, b_forget, rel_bias,
           w_fox_out, w_dsa_out, w_out, g_post):
    bsz, seq, d = x_prompt.shape
    dbsz, t_new, _ = x_sample.shape
    depth, n_phys, page, nh, dh = cache_fox_k.shape
    di = cache_idx_k.shape[-1]
    kv_t = lambda a: a.transpose(0, 1, 3, 4, 2).reshape(depth, n_phys, nh * dh, page)
    cache_fox_k, cache_fox_v, cache_dsa_k, cache_dsa_v = map(kv_t, (cache_fox_k, cache_fox_v, cache_dsa_k, cache_dsa_v))
    cache_idx_k = cache_idx_k.transpose(0, 1, 3, 2)
    wf = nh * dh
    nhi = w_in.shape[-1] - (8 * wf + nh + di + 2 * d)
    nhi = nhi // (di + 1)
    n_pages = page_table.shape[1]
    past = n_pages * page
    n_buckets = rel_bias.shape[0]
    attn_scale = dh ** -0.5
    idx_w_scale = (nhi * di) ** -0.5
    geom = dict(nh=nh, dh=dh, nhi=nhi, di=di, attn_scale=attn_scale, idx_w_scale=idx_w_scale)
    assert nh == 8 and nhi == 8 and page == LANES

    new_p = [[] for _ in range(6)]
    new_s = [[] for _ in range(6)]
    h_p, h_s = x_prompt, x_sample
    n_c = bsz + dbsz
    n_c_pad = -(-n_c // 8) * 8
    c_all = jnp.pad(jnp.concatenate([c_prompt, c_sample], axis=0), ((0, n_c_pad - n_c), (0, 0)))
    topk_s = min(TOPK_MAX, (past + t_new) // 4)
    lpad = past + 4 * LANES

    for l in range(depth):
        mod = _mod(c_all, w_mod[l], b_mod[l])
        shift, scale, gate = mod[:, :d], mod[:, d:2 * d], mod[:, 2 * d:]
        rep = lambda a: jnp.repeat(a[bsz:n_c], t_new, axis=0)

        x2 = h_p.reshape(bsz * seq, d)
        (fk, fv, lf, dk, dv, ik, fz, dz, gf, gd, iw, qa, ka, fv_h, dq_h, dk_h, dv_h, iq_h, ik_b) = _proj(
            x2, scale[:bsz, None, :], shift[:bsz, None, :], g_pre[l], w_in[l], b_forget[l],
            prompt=True, bsz=bsz, seq=seq, **geom)
        o_fox = _fox_prompt(qa, ka, fv_h)
        o_dsa = _dsa_prompt(iq_h, iw, ik_b, dq_h, dk_h, dv_h, rel_bias)
        y2 = _mix_out(x2, gate[:bsz, None, :], o_fox, fz, o_dsa, dz, gf, gd, w_fox_out[l], w_dsa_out[l], w_out[l],
                      g_post[l], head_major=True, seq=seq, nh=nh, dh=dh)
        h_p = y2.reshape(bsz, seq, d)
        for lst, a, shp in zip(new_p, (fk, fv, lf, dk, dv, ik),
                               ((nh, dh), (nh, dh), (nh,), (nh, dh), (nh, dh), (di,))):
            lst.append(a.reshape((bsz, seq) + shp))

        xs2 = h_s.reshape(dbsz * t_new, d)
        (sfk, sfv, slf, sdk, sdv, sik, sfz, sdz, sgf, sgd, siw, sfq, sdq, siq, scn) = _proj(
            xs2, rep(scale), rep(shift), g_pre[l], w_in[l], b_forget[l],
            prompt=False, bsz=dbsz, seq=t_new, **geom)
        q32 = siq.reshape(dbsz, t_new * nhi, di)
        w32 = jnp.broadcast_to(siw.reshape(dbsz, t_new * nhi, 1), (dbsz, t_new * nhi, LANES))
        pad_new = lambda a: jnp.pad(a.reshape(dbsz, t_new, -1).transpose(0, 2, 1), ((0, 0), (0, 0), (0, page - t_new)))
        scores = _sample_idx_scores(page_table, q32, w32, pad_new(sik), cache_idx_k, l, lpad)
        sel = _sample_select(scores.reshape(dbsz * t_new, lpad), t_new, past, topk_s).reshape(dbsz, t_new, lpad)
        tpos = jnp.arange(t_new)
        rel = past + tpos[:, None] - jnp.arange(past + page)[None, :]
        t5 = rel_bias[_t5_bucket(rel, n_buckets)].astype(F32)
        t5 = t5.transpose(0, 2, 1).reshape(t_new * nh, past + page)
        ppc = 8
        while n_pages % ppc:
            ppc //= 2
        nc = n_pages // ppc
        t5c = t5[:, :past].reshape(t_new * nh, nc, ppc * page).transpose(1, 0, 2)
        t5n = t5[:, past:]
        selb = sel[:, :, :past].reshape(dbsz, t_new, nc, ppc * page).transpose(0, 2, 1, 3)
        seln = sel[:, :, past:past + page]
        cn = scn[:, :nh].reshape(dbsz, t_new, nh)
        cn32 = jnp.broadcast_to(cn.reshape(dbsz, t_new * nh, 1), (dbsz, t_new * nh, LANES))
        dcn = cn[:, :, None, :] - cn[:, None, :, :]
        dcn = jnp.where((tpos[None, :] <= tpos[:, None])[None, :, :, None], dcn, NEG_INF)
        bnf = jnp.pad(dcn.transpose(0, 1, 3, 2).reshape(dbsz, t_new * nh, t_new), ((0, 0), (0, 0), (0, LANES - t_new)),
                      constant_values=NEG_INF)
        wt = _page_suffix(cache_fox_logf[l].transpose(0, 2, 1))
        so_fox, so_dsa = _sample_attention(
            page_table, l, sfq.reshape(dbsz, t_new, wf), sdq.reshape(dbsz, t_new, wf),
            pad_new(sfk), pad_new(sfv), pad_new(sdk), pad_new(sdv), cn32, bnf, selb, seln, t5c, t5n,
            cache_fox_k, cache_fox_v, cache_dsa_k, cache_dsa_v, wt, nh=nh, dh=dh)
        ys2 = _mix_out(xs2, rep(gate), so_fox.reshape(dbsz * t_new, wf), sfz, so_dsa.reshape(dbsz * t_new, wf), sdz,
                       sgf, sgd, w_fox_out[l], w_dsa_out[l], w_out[l], g_post[l], head_major=False, seq=t_new,
                       nh=nh, dh=dh)
        h_s = ys2.reshape(dbsz, t_new, d)
        for lst, a, shp in zip(new_s, (sfk, sfv, slf, sdk, sdv, sik),
                               ((nh, dh), (nh, dh), (nh,), (nh, dh), (nh, dh), (di,))):
            lst.append(a.reshape((dbsz, t_new) + shp))

    return (h_p, h_s) + tuple(jnp.stack(v) for v in new_p) + tuple(jnp.stack(v) for v in new_s)
```

```python
import functools

import numpy as np
import jax
import jax.numpy as jnp
from jax import lax
from jax.experimental import pallas as pl
from jax.experimental.pallas import tpu as pltpu

F32 = jnp.float32
BF16 = jnp.bfloat16
I32 = jnp.int32

LANES = 128
TOPK_MAX = 256
MAX_DISTANCE = 128
RMS_EPS = 1e-6
NEG_INF = float("-inf")
M_INIT = -1e30
LOG2E = 1.4426950408889634
INT_MIN = -(2 ** 31)
INT_MAX = 2 ** 31 - 1
N_AUG = 3
VMEM_LIMIT = 60 * 1024 * 1024
HEADS_PER_STEP = 2

NT = (((1,), (1,)), ((), ()))


def _cparams(sem):
    return pltpu.CompilerParams(dimension_semantics=sem, vmem_limit_bytes=VMEM_LIMIT)


def _resident(block_shape, index_map):
    return pl.BlockSpec(block_shape, index_map, pipeline_mode=pl.Buffered(1))


def _split3(x):
    hi = x.astype(BF16)
    r1 = x - hi.astype(F32)
    mid = r1.astype(BF16)
    lo = (r1 - mid.astype(F32)).astype(BF16)
    return hi, mid, lo


def _dot(a, b):
    return jnp.dot(a, b, preferred_element_type=F32)


def _dot_nt(a, b):
    return lax.dot_general(a, b, NT, preferred_element_type=F32)


def _sigmoid(x):
    return 1.0 / (1.0 + jnp.exp(-x))


def _silu(x):
    return x * _sigmoid(x)


def _mod_kernel(c_ref, w_ref, b_ref, o_ref):
    a = _silu(c_ref[...])
    a_hi, a_mid, _ = _split3(a)
    w_hi, w_mid, _ = _split3(w_ref[...])
    o_ref[...] = _dot(a_hi, w_hi) + (_dot(a_hi, w_mid) + _dot(a_mid, w_hi)) + b_ref[...]


def _mod(c, w_mod, b_mod):
    n, d = c.shape
    e = w_mod.shape[1]
    te = 512
    assert e % te == 0 and n % 8 == 0
    return pl.pallas_call(
        _mod_kernel,
        grid=(e // te,),
        in_specs=[pl.BlockSpec((n, d), lambda j: (0, 0)),
                  pl.BlockSpec((d, te), lambda j: (0, j)),
                  pl.BlockSpec((1, te), lambda j: (0, j))],
        out_specs=pl.BlockSpec((n, te), lambda j: (0, j)),
        out_shape=jax.ShapeDtypeStruct((n, e), F32),
        compiler_params=_cparams(("arbitrary",)),
        name="mod",
    )(c, w_mod, b_mod.reshape(1, e))


def _proj_kernel(x_ref, sc_ref, sh_ref, g_ref, w_ref, bf_ref, pl_ref, *refs,
                 cols, prompt, group, tm, nh, dh, nhi, di, idx_w_scale):
    if prompt:
        (fk_o, fv_o, lf_o, dk_o, dv_o, ik_o, fz_o, dz_o, gf_o, gd_o, iw_o,
         qa_o, ka_o, fvh_o, dqh_o, dkh_o, dvh_o, iqh_o, ikb_o, carry) = refs
    else:
        (fk_o, fv_o, lf_o, dk_o, dv_o, ik_o, fz_o, dz_o, gf_o, gd_o, iw_o,
         fq_o, dq_o, iq_o, cn_o, carry) = refs
    i = pl.program_id(0)
    x = x_ref[...]
    y = x * lax.rsqrt(jnp.mean(x * x, axis=-1, keepdims=True) + RMS_EPS) * g_ref[...]
    hb = (y * (1.0 + sc_ref[...]) + sh_ref[...]).astype(BF16)

    def mm(name):
        c0, c1 = cols[name]
        return _dot(hb, w_ref[:, c0:c1])

    fk = mm("fk"); fk_o[...] = fk
    fv = mm("fv"); fv_o[...] = fv
    dk = mm("dk"); dk_o[...] = dk
    dv = mm("dv"); dv_o[...] = dv
    fz_o[...] = mm("fz")
    dz_o[...] = mm("dz")
    gf_o[...] = mm("gf")
    gd_o[...] = mm("gd")
    small = mm("small")
    ik = small[:, LANES - di:]
    ik_o[...] = ik
    iw_o[...] = small[:, 8:8 + nhi] * idx_w_scale
    z = small + bf_ref[...]
    logf_full = jnp.minimum(z, 0.0) - jnp.log(1.0 + jnp.exp(-jnp.abs(z)))
    lane = lax.broadcasted_iota(I32, (tm, LANES), 1)
    logf = jnp.where(lane < nh, logf_full, 0.0)
    lf_o[...] = logf[:, :nh]

    r = lax.broadcasted_iota(I32, (tm, tm), 0)
    c = lax.broadcasted_iota(I32, (tm, tm), 1)
    g_in = min(group, tm)
    tri = jnp.where((c <= r) & ((c // g_in) == (r // g_in)), 1.0, 0.0).astype(BF16)
    l_hi, l_mid, l_lo = _split3(logf)
    cum = _dot(tri, l_lo) + _dot(tri, l_mid) + _dot(tri, l_hi)
    if group > tm:
        @pl.when((i * tm) % group == 0)
        def _():
            carry[...] = jnp.zeros_like(carry)
        cum = cum + carry[...]
        carry[...] = cum[tm - 1:tm, :]

    if prompt:
        c_hi, c_mid, c_lo = _split3(cum * LOG2E)
        aug = _dot(c_hi, pl_ref[0]) + _dot(c_mid, pl_ref[1]) + _dot(c_lo, pl_ref[2]) + pl_ref[3][0:1, :].astype(F32)
        wa = nh * LANES
        qa = mm("qa") + aug[:, :wa]
        ka = mm("ka") + aug[:, wa:]
        dq = mm("dq"); iq = mm("iq")
        ones_pad = jnp.where(lax.broadcasted_iota(I32, (tm, LANES - dh), 1) == 0, 1.0, 0.0)
        for h in range(nh):
            qa_o[h] = qa[:, h * LANES:(h + 1) * LANES].astype(BF16)
            ka_o[h] = ka[:, h * LANES:(h + 1) * LANES].astype(BF16)
            fvh_o[h] = jnp.concatenate([fv[:, h * dh:(h + 1) * dh], ones_pad], axis=1).astype(BF16)
            dqh_o[h] = dq[:, h * dh:(h + 1) * dh].astype(BF16)
            dkh_o[h] = dk[:, h * dh:(h + 1) * dh].astype(BF16)
            dvh_o[h] = jnp.concatenate([dv[:, h * dh:(h + 1) * dh], ones_pad], axis=1).astype(BF16)
        for h in range(nhi):
            iqh_o[h] = iq[:, h * di:(h + 1) * di].astype(BF16)
        ikb_o[...] = ik.astype(BF16)
    else:
        fq_o[...] = mm("fq").astype(BF16)
        dq_o[...] = mm("dq").astype(BF16)
        iq_o[...] = mm("iq").astype(BF16)
        cn_o[...] = cum


def _proj_weights(w_in, b_forget, prompt, nh, dh, nhi, di, d, attn_scale):
    wf = nh * dh
    sizes = [wf, wf, wf, nh, wf, wf, wf, wf, nhi * di, di, nhi, wf, d, d]
    offs = np.cumsum([0] + sizes)
    fq, fk, fv, ff, fz, dq, dk, dv, iq, ik, iw, dz, gf, gd = [w_in[:, offs[j]:offs[j + 1]] for j in range(14)]
    fq = fq * (attn_scale * (LOG2E if prompt else 1.0))
    dq = dq * (attn_scale * (LOG2E if prompt else 1.0))
    small = jnp.zeros((d, LANES), F32)
    small = small.at[:, 0:nh].set(ff).at[:, 8:8 + nhi].set(iw).at[:, LANES - di:].set(ik)

    def pad_heads(w):
        return jnp.pad(w.reshape(d, nh, dh), ((0, 0), (0, 0), (0, LANES - dh))).reshape(d, nh * LANES)

    if prompt:
        groups = [("qa", pad_heads(fq)), ("ka", pad_heads(fk))]
    else:
        groups = [("fq", fq)]
    groups += [("fk", fk), ("fv", fv), ("fz", fz), ("dq", dq), ("dk", dk), ("dv", dv), ("iq", iq), ("dz", dz),
               ("gf", gf), ("gd", gd), ("small", small)]
    cols, c0 = {}, 0
    for name, w in groups:
        cols[name] = (c0, c0 + w.shape[1])
        c0 += w.shape[1]
    w_all = jnp.concatenate([w for _, w in groups], axis=1).astype(BF16)
    bf_row = jnp.zeros((1, LANES), F32).at[0, 0:nh].set(b_forget)
    return w_all, cols, bf_row


def _placement(nh, dh):
    wa = nh * LANES
    p = np.zeros((N_AUG + 1, LANES, 2 * wa), np.float32)
    for h in range(nh):
        for c in range(N_AUG):
            p[c, h, h * LANES + dh + c] = 1.0
            p[c, h, wa + h * LANES + dh + N_AUG + c] = -1.0
            p[N_AUG, :, h * LANES + dh + N_AUG + c] = 1.0
            p[N_AUG, :, wa + h * LANES + dh + c] = 1.0
    return jnp.asarray(p, BF16)


def _proj(x2, scale, shift, g_pre, w_in, b_forget, *, prompt, bsz, seq, nh, dh, nhi, di, attn_scale, idx_w_scale):
    n, d = x2.shape
    wf = nh * dh
    assert dh + 2 * N_AUG <= LANES and nh <= 8 and nhi <= 8 and di <= LANES - 16
    w_all, cols, bf_row = _proj_weights(w_in, b_forget, prompt, nh, dh, nhi, di, d, attn_scale)
    place = _placement(nh, dh)
    tm = 256 if prompt else min(n, 256)
    assert n % tm == 0 and seq % min(seq, tm) == 0
    nt = seq // tm if prompt else 1
    row = lambda i: (i, 0)
    if prompt:
        mod_spec = pl.BlockSpec((None, 1, d), lambda i: (i // nt, 0, 0))
        hm = lambda i: (i // nt, 0, i % nt, 0)
    else:
        mod_spec = pl.BlockSpec((tm, d), row)
    f32o = lambda w: (jax.ShapeDtypeStruct((n, w), F32), pl.BlockSpec((tm, w), row))
    outs = [f32o(wf), f32o(wf), f32o(nh), f32o(wf), f32o(wf), f32o(di), f32o(wf), f32o(wf), f32o(d), f32o(d), f32o(nhi)]
    if prompt:
        hm_o = lambda heads, w: (jax.ShapeDtypeStruct((bsz, heads, seq, w), BF16), pl.BlockSpec((None, heads, tm, w), hm))
        outs += [hm_o(nh, LANES), hm_o(nh, LANES), hm_o(nh, LANES), hm_o(nh, dh), hm_o(nh, dh), hm_o(nh, LANES), hm_o(nhi, di),
                 (jax.ShapeDtypeStruct((bsz, seq, di), BF16), pl.BlockSpec((None, tm, di), lambda i: (i // nt, i % nt, 0)))]
    else:
        b16o = lambda w: (jax.ShapeDtypeStruct((n, w), BF16), pl.BlockSpec((tm, w), row))
        outs += [b16o(wf), b16o(wf), b16o(nhi * di), f32o(LANES)]
    kern = functools.partial(_proj_kernel, cols=cols, prompt=prompt, group=seq, tm=tm, nh=nh, dh=dh, nhi=nhi, di=di,
                             idx_w_scale=idx_w_scale)
    return pl.pallas_call(
        kern,
        grid=(n // tm,),
        in_specs=[pl.BlockSpec((tm, d), row), mod_spec, mod_spec,
                  _resident((1, d), lambda i: (0, 0)),
                  _resident(w_all.shape, lambda i: (0, 0)),
                  _resident((1, LANES), lambda i: (0, 0)),
                  _resident(place.shape, lambda i: (0, 0, 0))],
        out_specs=[o[1] for o in outs],
        out_shape=[o[0] for o in outs],
        scratch_shapes=[pltpu.VMEM((1, LANES), F32)],
        compiler_params=_cparams(("arbitrary",)),
        name="proj_prompt" if prompt else "proj_sample",
    )(x2, scale, shift, g_pre.reshape(1, d), w_all, bf_row, place)


def _attend(carry, s, v):
    m, l, acc = carry
    m_new = jnp.maximum(m, jnp.max(s, axis=1, keepdims=True))
    alpha = jnp.exp(m - m_new)
    p = jnp.exp(s - m_new)
    l = alpha * l + jnp.sum(p, axis=1, keepdims=True)
    acc = alpha * acc + _dot(p.astype(BF16), v)
    return m_new, l, acc


def _attend_init(rows, dv):
    return (jnp.full((rows, 1), M_INIT, F32), jnp.zeros((rows, 1), F32), jnp.zeros((rows, dv), F32))


def _fold_lanes(x, op):
    out = x[:, :LANES]
    for j in range(1, x.shape[1] // LANES):
        out = op(out, x[:, j * LANES:(j + 1) * LANES])
    return out


def _fox_kernel(q_ref, k_ref, v_ref, o_ref, s_ref, mx_ref, acc_ref, *, tq, cw):
    qi = pl.program_id(2)
    nfull = (qi * tq) // cw
    hs = range(HEADS_PER_STEP)
    q = [q_ref[g] for g in hs]

    def logits(g, c):
        return _dot_nt(q[g], k_ref[g, pl.ds(pl.multiple_of(c * cw, cw), cw), :])

    mx_ref[...] = jnp.full(mx_ref.shape, M_INIT, F32)

    def p1(c, _):
        for g in hs:
            s = logits(g, c)
            s_ref[g, c] = s
            mx_ref[g] = jnp.maximum(mx_ref[g], _fold_lanes(s, jnp.maximum))
        return 0

    lax.fori_loop(0, nfull, p1, 0)
    row = qi * tq + lax.broadcasted_iota(I32, (tq, cw), 0)
    col = nfull * cw + lax.broadcasted_iota(I32, (tq, cw), 1)
    m = []
    for g in hs:
        s = jnp.where(col <= row, logits(g, nfull), NEG_INF)
        s_ref[g, nfull] = s
        m.append(jnp.max(jnp.maximum(mx_ref[g], _fold_lanes(s, jnp.maximum)), axis=1, keepdims=True))

    acc_ref[...] = jnp.zeros(acc_ref.shape, F32)

    def p2(c, _):
        for g in hs:
            p = jnp.exp2(s_ref[g, c] - m[g])
            acc_ref[g] += _dot(p.astype(BF16), v_ref[g, pl.ds(pl.multiple_of(c * cw, cw), cw), :])
        return 0

    lax.fori_loop(0, nfull + 1, p2, 0)
    dv = o_ref.shape[-1]
    for g in hs:
        acc = acc_ref[g]
        o_ref[g] = acc[:, :dv] / acc[:, dv:dv + 1]


def _fox_prompt(qa, ka, fv1, dv):
    bsz, nh, seq, wa = qa.shape
    wv = fv1.shape[-1]
    tq = min(256, seq)
    cw = min(1024, seq)
    g = HEADS_PER_STEP
    assert seq % cw == 0 and cw % tq == 0 and nh % g == 0 and dv < wv
    return pl.pallas_call(
        functools.partial(_fox_kernel, tq=tq, cw=cw),
        grid=(bsz, nh // g, seq // tq),
        in_specs=[pl.BlockSpec((None, g, tq, wa), lambda b, h, i: (b, h, i, 0)),
                  pl.BlockSpec((None, g, seq, wa), lambda b, h, i: (b, h, 0, 0)),
                  pl.BlockSpec((None, g, seq, wv), lambda b, h, i: (b, h, 0, 0))],
        out_specs=pl.BlockSpec((None, g, tq, dv), lambda b, h, i: (b, h, i, 0)),
        out_shape=jax.ShapeDtypeStruct((bsz, nh, seq, dv), F32),
        scratch_shapes=[pltpu.VMEM((g, seq // cw, tq, cw), F32), pltpu.VMEM((g, tq, LANES), F32),
                        pltpu.VMEM((g, tq, wv), F32)],
        compiler_params=_cparams(("arbitrary", "arbitrary", "arbitrary")),
        name="fox_prompt",
    )(qa, ka, fv1)


def _sort_key(x):
    bits = lax.bitcast_convert_type(x, I32)
    return jnp.where(bits < 0, bits ^ jnp.int32(INT_MAX), bits)


def _select_topk(sc_ref, nslab4, rows, kk, idx_bits):
    lane = lax.broadcasted_iota(I32, (rows, LANES), 1)
    kf = jnp.float32(kk)

    def count(pred):
        def body(c4, cnt):
            for j in range(4):
                s = c4 * 4 + j
                cnt = cnt + jnp.where(pred(sc_ref[s], s), 1.0, 0.0)
            return cnt
        cnt = lax.fori_loop(0, nslab4, body, jnp.zeros((rows, LANES), F32))
        return jnp.sum(cnt, axis=1, keepdims=True)

    def bit_step(i, thr):
        cand = thr + jnp.left_shift(jnp.int32(1), 31 - i)
        c = count(lambda k, s: k >= cand)
        return jnp.where(c >= kf, cand, thr)

    thr = lax.fori_loop(0, 32, bit_step, jnp.full((rows, 1), INT_MIN, I32))
    thr = jnp.maximum(thr, INT_MIN + 1)
    c_gt = count(lambda k, s: k > thr)
    c_ge = count(lambda k, s: k >= thr)
    need = kf - c_gt

    def idx_step(i, cut):
        cand = cut + jnp.left_shift(jnp.int32(1), idx_bits - 1 - i)
        c = count(lambda k, s: (k == thr) & ((s * LANES + lane) < cand))
        return jnp.where(c < need, cand, cut)

    surplus = c_ge > kf
    cut = lax.cond(jnp.max(jnp.where(surplus, 1.0, 0.0)) > 0.0,
                   lambda: lax.fori_loop(0, idx_bits, idx_step, jnp.zeros((rows, 1), I32)),
                   lambda: jnp.zeros((rows, 1), I32))
    cut = jnp.where(surplus, cut, INT_MAX)

    def fin(c4, _):
        for j in range(4):
            s = c4 * 4 + j
            k = sc_ref[s]
            sel = (k > thr) | ((k == thr) & ((s * LANES + lane) <= cut))
            sc_ref[s] = lax.bitcast_convert_type(jnp.where(sel, 0.0, NEG_INF), I32)
        return 0

    lax.fori_loop(0, nslab4, fin, 0)


def _mask_slab(sc_ref, s):
    return lax.bitcast_convert_type(sc_ref[s], F32)


def _dsa_kernel(iq_ref, iw_ref, ik_ref, dq_ref, dk_ref, dv_ref, tab_ref, o_ref, sc_ref, s_ref, mx_ref, acc_ref,
                *, tq, topk, nh, nhi, idx_bits, spc):
    qi = pl.program_id(1)
    nch = qi // spc + 1
    cw = spc * LANES
    sw = 4 * LANES
    w = iw_ref[...]
    row = qi * tq + lax.broadcasted_iota(I32, (tq, sw), 0)

    def scores(c, _):
        k = ik_ref[pl.ds(pl.multiple_of(c * sw, sw), sw), :]
        acc = jnp.zeros((tq, sw), F32)
        for h in range(nhi):
            acc = acc + jnp.maximum(_dot_nt(iq_ref[h], k), 0.0) * w[:, h:h + 1]
        col = c * sw + lax.broadcasted_iota(I32, (tq, sw), 1)
        key = jnp.where(col <= row, _sort_key(acc), INT_MIN)
        for j in range(4):
            sc_ref[c * 4 + j] = key[:, j * LANES:(j + 1) * LANES]
        return 0

    lax.fori_loop(0, nch * (spc // 4), scores, 0)
    _select_topk(sc_ref, nch * (spc // 4), tq, topk, idx_bits)

    hs = range(HEADS_PER_STEP)
    dv = o_ref.shape[-1]

    def heads(hp, _):
        h0 = hp * HEADS_PER_STEP
        q = [dq_ref[h0 + g] for g in hs]
        mx_ref[...] = jnp.full(mx_ref.shape, M_INIT, F32)

        def p1(c, _):
            ks = pl.ds(pl.multiple_of(c * cw, cw), cw)
            mask = [_mask_slab(sc_ref, c * spc + j) for j in range(spc)]
            tsel = [jnp.clip(qi - (c * spc + j), 0, 2) for j in range(spc)]
            for g in hs:
                bias = jnp.concatenate([mask[j] + tab_ref[tsel[j], h0 + g] for j in range(spc)], axis=1)
                s = _dot_nt(q[g], dk_ref[h0 + g, ks, :]) + bias
                s_ref[g, c] = s
                mx_ref[g] = jnp.maximum(mx_ref[g], _fold_lanes(s, jnp.maximum))
            return 0

        lax.fori_loop(0, nch, p1, 0)
        m = [jnp.max(mx_ref[g], axis=1, keepdims=True) for g in hs]
        acc_ref[...] = jnp.zeros(acc_ref.shape, F32)

        def p2(c, _):
            ks = pl.ds(pl.multiple_of(c * cw, cw), cw)
            for g in hs:
                p = jnp.exp2(s_ref[g, c] - m[g])
                acc_ref[g] += _dot(p.astype(BF16), dv_ref[h0 + g, ks, :])
            return 0

        lax.fori_loop(0, nch, p2, 0)
        for g in hs:
            acc = acc_ref[g]
            o_ref[h0 + g] = acc[:, :dv] / acc[:, dv:dv + 1]
        return 0

    lax.fori_loop(0, nh // HEADS_PER_STEP, heads, 0)


def _t5_bucket(rel, n_buckets):
    max_exact = n_buckets // 2
    n = jnp.maximum(rel, 0)
    nf = jnp.maximum(n, 1).astype(F32)
    large = max_exact + (jnp.log(nf / max_exact) / np.log(MAX_DISTANCE / max_exact) * (n_buckets - max_exact)).astype(I32)
    return jnp.where(n < max_exact, n, jnp.minimum(large, n_buckets - 1))


def _dsa_prompt(iq_h, iw, ik_b, dq_h, dk_h, dv1_h, rel_bias):
    bsz, nh, seq, dh = dq_h.shape
    nhi, di = iq_h.shape[1], iq_h.shape[3]
    wv = dv1_h.shape[-1]
    n_buckets = rel_bias.shape[0]
    tq = LANES
    spc = 8
    assert seq % (spc * LANES) == 0 and tq == MAX_DISTANCE and dh < wv
    topk = min(TOPK_MAX, seq // 4)
    i = jnp.arange(tq)
    rel = jnp.arange(2)[:, None, None] * tq + i[None, :, None] - i[None, None, :]
    far = rel_bias[n_buckets - 1].astype(F32)
    tab = rel_bias[_t5_bucket(rel, n_buckets)].astype(F32) - far
    tab = jnp.concatenate([tab.transpose(0, 3, 1, 2) * LOG2E, jnp.zeros((1, nh, tq, tq), F32)], axis=0)
    nslab = seq // LANES
    kern = functools.partial(_dsa_kernel, tq=tq, topk=topk, nh=nh, nhi=nhi, idx_bits=max(1, (seq - 1).bit_length()),
                             spc=spc)
    return pl.pallas_call(
        kern,
        grid=(bsz, seq // tq),
        in_specs=[pl.BlockSpec((None, nhi, tq, di), lambda b, i: (b, 0, i, 0)),
                  pl.BlockSpec((None, tq, nhi), lambda b, i: (b, i, 0)),
                  _resident((None, seq, di), lambda b, i: (b, 0, 0)),
                  pl.BlockSpec((None, nh, tq, dh), lambda b, i: (b, 0, i, 0)),
                  _resident((None, nh, seq, dh), lambda b, i: (b, 0, 0, 0)),
                  _resident((None, nh, seq, wv), lambda b, i: (b, 0, 0, 0)),
                  _resident(tab.shape, lambda b, i: (0, 0, 0, 0))],
        out_specs=pl.BlockSpec((None, nh, tq, dh), lambda b, i: (b, 0, i, 0)),
        out_shape=jax.ShapeDtypeStruct((bsz, nh, seq, dh), F32),
        scratch_shapes=[pltpu.VMEM((nslab, tq, LANES), I32),
                        pltpu.VMEM((HEADS_PER_STEP, nslab // spc, tq, spc * LANES), F32),
                        pltpu.VMEM((HEADS_PER_STEP, tq, LANES), F32), pltpu.VMEM((HEADS_PER_STEP, tq, wv), F32)],
        compiler_params=_cparams(("arbitrary", "arbitrary")),
        name="dsa_prompt",
    )(iq_h, iw.reshape(bsz, seq, nhi), ik_b, dq_h, dk_h, dv1_h, tab)


def _out_kernel(x_ref, gate_ref, of_ref, fz_ref, od_ref, dz_ref, gf_ref, gd_ref, wf_ref, wd_ref, wo_ref, gp_ref, y_ref,
                *, head_major, nh, dh):
    def branch(o_ref, z_ref, w_ref):
        g = _silu(z_ref[...])
        if head_major:
            a = None
            for h in range(nh):
                t = _dot((o_ref[h] * g[:, h * dh:(h + 1) * dh]).astype(BF16), w_ref[h * dh:(h + 1) * dh, :])
                a = t if a is None else a + t
            return a
        return _dot((o_ref[...] * g).astype(BF16), w_ref[...])

    a = branch(of_ref, fz_ref, wf_ref)
    b = branch(od_ref, dz_ref, wd_ref)
    merged = _sigmoid(gf_ref[...]) * a + _sigmoid(gd_ref[...]) * b
    out = _dot(merged.astype(BF16), wo_ref[...])
    nrm = out * lax.rsqrt(jnp.mean(out * out, axis=-1, keepdims=True) + RMS_EPS) * gp_ref[...]
    y_ref[...] = x_ref[...] + gate_ref[...] * nrm


def _mix_out(x2, gate, o_fox, fz, o_dsa, dz, gf, gd, w_fox_out, w_dsa_out, w_out, g_post, *, head_major, seq, nh, dh):
    n, d = x2.shape
    wf = nh * dh
    tm = min(256, n)
    assert n % tm == 0
    row = lambda i: (i, 0)
    if head_major:
        nt = seq // tm
        o_spec = pl.BlockSpec((None, nh, tm, dh), lambda i: (i // nt, 0, i % nt, 0))
        gate_spec = pl.BlockSpec((None, 1, d), lambda i: (i // nt, 0, 0))
    else:
        o_spec = pl.BlockSpec((tm, wf), row)
        gate_spec = pl.BlockSpec((tm, d), row)
    return pl.pallas_call(
        functools.partial(_out_kernel, head_major=head_major, nh=nh, dh=dh),
        grid=(n // tm,),
        in_specs=[pl.BlockSpec((tm, d), row), gate_spec, o_spec, pl.BlockSpec((tm, wf), row),
                  o_spec, pl.BlockSpec((tm, wf), row), pl.BlockSpec((tm, d), row), pl.BlockSpec((tm, d), row),
                  _resident((wf, d), lambda i: (0, 0)), _resident((wf, d), lambda i: (0, 0)),
                  _resident((d, d), lambda i: (0, 0)), _resident((1, d), lambda i: (0, 0))],
        out_specs=pl.BlockSpec((tm, d), row),
        out_shape=jax.ShapeDtypeStruct((n, d), F32),
        compiler_params=_cparams(("arbitrary",)),
        name="mix_out_prompt" if head_major else "mix_out_sample",
    )(x2, gate, o_fox, fz, o_dsa, dz, gf, gd, w_fox_out.astype(BF16), w_dsa_out.astype(BF16), w_out.astype(BF16),
      g_post.reshape(1, d))


def _sidx_kernel(pt_ref, q_ref, w_ref, kn_ref, *refs, n_pages, t_new, past):
    pages = refs[:n_pages]
    o_ref = refs[n_pages]
    q = q_ref[...]
    w = w_ref[...]

    def slab(kt):
        r = jnp.maximum(_dot(q, kt.astype(BF16)), 0.0) * w
        return jnp.concatenate([jnp.sum(r[8 * t:8 * t + 8], axis=0, keepdims=True) for t in range(t_new)], axis=0)

    for j in range(n_pages):
        o_ref[:, j * LANES:(j + 1) * LANES] = slab(pages[j][...])
    new = slab(kn_ref[...])
    lane = lax.broadcasted_iota(I32, (t_new, LANES), 1)
    trow = lax.broadcasted_iota(I32, (t_new, LANES), 0)
    o_ref[:, past:past + LANES] = jnp.where(lane <= trow, new, NEG_INF)
    pad = o_ref.shape[-1] - past - LANES
    if pad:
        o_ref[:, past + LANES:] = jnp.full((t_new, pad), NEG_INF, F32)


def _sample_idx_scores(page_table, q32, w32, ik_new_pad, cache_idx_k, layer, lpad):
    dbsz, n_pages = page_table.shape
    t_new = q32.shape[1] // 8
    di = q32.shape[2]
    page = cache_idx_k.shape[3]
    assert page == LANES
    page_specs = [pl.BlockSpec((None, None, di, page), (lambda b, pt, j=j: (layer, pt[b, j], 0, 0))) for j in range(n_pages)]
    return pl.pallas_call(
        functools.partial(_sidx_kernel, n_pages=n_pages, t_new=t_new, past=n_pages * page),
        grid_spec=pltpu.PrefetchScalarGridSpec(
            num_scalar_prefetch=1,
            grid=(dbsz,),
            in_specs=[pl.BlockSpec((None, t_new * 8, di), lambda b, pt: (b, 0, 0)),
                      pl.BlockSpec((None, t_new * 8, LANES), lambda b, pt: (b, 0, 0)),
                      pl.BlockSpec((None, di, page), lambda b, pt: (b, 0, 0))] + page_specs,
            out_specs=pl.BlockSpec((None, t_new, lpad), lambda b, pt: (b, 0, 0)),
        ),
        out_shape=jax.ShapeDtypeStruct((dbsz, t_new, lpad), F32),
        compiler_params=_cparams(("arbitrary",)),
        name="sample_idx_scores",
    )(page_table, q32, w32, ik_new_pad, *([cache_idx_k] * n_pages))


def _ssel_kernel(s_ref, o_ref, sc_ref, *, rows, t_new, past, topk, idx_bits):
    nslab = sc_ref.shape[0]
    lane = lax.broadcasted_iota(I32, (rows, LANES), 1)
    t = lax.broadcasted_iota(I32, (rows, LANES), 0) % t_new
    for s in range(nslab):
        valid = (s * LANES + lane) <= past + t
        sc_ref[s] = jnp.where(valid, _sort_key(s_ref[:, s * LANES:(s + 1) * LANES]), INT_MIN)
    _select_topk(sc_ref, nslab // 4, rows, topk, idx_bits)
    for s in range(nslab):
        o_ref[:, s * LANES:(s + 1) * LANES] = _mask_slab(sc_ref, s)


def _sample_select(scores2, t_new, past, topk):
    n, lpad = scores2.shape
    rows = min(n, LANES)
    assert n % rows == 0 and rows % t_new == 0 and lpad % (4 * LANES) == 0
    nslab = lpad // LANES
    return pl.pallas_call(
        functools.partial(_ssel_kernel, rows=rows, t_new=t_new, past=past, topk=topk, idx_bits=max(1, (lpad - 1).bit_length())),
        grid=(n // rows,),
        in_specs=[pl.BlockSpec((rows, lpad), lambda i: (i, 0))],
        out_specs=pl.BlockSpec((rows, lpad), lambda i: (i, 0)),
        out_shape=jax.ShapeDtypeStruct((n, lpad), F32),
        scratch_shapes=[pltpu.VMEM((nslab, rows, LANES), I32)],
        compiler_params=_cparams(("arbitrary",)),
        name="sample_select",
    )(scores2)


def _wt_kernel(lp_ref, o_ref):
    pb = lp_ref.shape[0]
    lp = lp_ref[...].reshape(pb * 8, LANES)
    r = lax.broadcasted_iota(I32, (LANES, 2 * LANES), 0)
    c = lax.broadcasted_iota(I32, (LANES, 2 * LANES), 1)
    u = jnp.where((c >= LANES) | (r > c), 1.0, 0.0).astype(BF16)
    hi, mid, lo = _split3(lp)
    res = _dot(lo, u) + _dot(mid, u) + _dot(hi, u)
    o_ref[:, 0:8, :] = res[:, :LANES].reshape(pb, 8, LANES)
    o_ref[:, 8:16, :] = res[:, LANES:].reshape(pb, 8, LANES)


def _page_suffix(logf_t):
    n_phys = logf_t.shape[0]
    pb = 256
    while n_phys % pb:
        pb //= 2
    return pl.pallas_call(
        _wt_kernel,
        grid=(n_phys // pb,),
        in_specs=[pl.BlockSpec((pb, 8, LANES), lambda i: (i, 0, 0))],
        out_specs=pl.BlockSpec((pb, 16, LANES), lambda i: (i, 0, 0)),
        out_shape=jax.ShapeDtypeStruct((n_phys, 16, LANES), F32),
        compiler_params=_cparams(("arbitrary",)),
        name="page_suffix",
    )(logf_t)


def _sattn_kernel(pt_ref, qf_ref, qd_ref, knf_ref, vnf_ref, knd_ref, vnd_ref, cn_ref, bnf_ref, selb_ref, seln_ref,
                  t5_ref, t5n_ref, *refs, ppc, nc, t_new, nh, dh):
    kf_p, vf_p, kd_p, vd_p, wt_p = (refs[i * ppc:(i + 1) * ppc] for i in range(5))
    of_ref, od_ref = refs[5 * ppc:5 * ppc + 2]
    qbf, qbd, mf, lf, af, md, ld, ad, car = refs[5 * ppc + 2:]
    c = pl.program_id(1)
    rows = t_new * 8
    wf = nh * dh
    headmask = (lax.broadcasted_iota(I32, (8, wf), 1) // dh) == lax.broadcasted_iota(I32, (8, wf), 0)

    def expand_rows(x):
        return jnp.concatenate([jnp.broadcast_to(x[t:t + 1], (8, x.shape[1])) for t in range(t_new)], axis=0)

    def step(state, q, ks, bias, vs):
        m_ref, l_ref, a_ref = state
        s = jnp.concatenate([_dot(q, k.astype(BF16)) for k in ks], axis=1) + bias
        m_old = m_ref[...]
        m_new = jnp.maximum(m_old, jnp.max(s, axis=1, keepdims=True))
        alpha = jnp.exp(m_old - m_new)
        p = jnp.exp(s - m_new)
        pb = p.astype(BF16)
        pv = None
        for j, v in enumerate(vs):
            t = _dot_nt(pb[:, j * LANES:(j + 1) * LANES], v.astype(BF16))
            pv = t if pv is None else pv + t
        m_ref[...] = m_new
        l_ref[...] = alpha * l_ref[...] + jnp.sum(p, axis=1, keepdims=True)
        a_ref[...] = alpha * a_ref[...] + pv

    @pl.when(c == 0)
    def _():
        hm = jnp.where(headmask, 1.0, 0.0)
        qbf[...] = (expand_rows(qf_ref[...].astype(F32)) * jnp.concatenate([hm] * t_new, axis=0)).astype(BF16)
        qbd[...] = (expand_rows(qd_ref[...].astype(F32)) * jnp.concatenate([hm] * t_new, axis=0)).astype(BF16)
        for m_ref, l_ref, a_ref in ((mf, lf, af), (md, ld, ad)):
            m_ref[...] = jnp.full(m_ref.shape, M_INIT, F32)
            l_ref[...] = jnp.zeros(l_ref.shape, F32)
            a_ref[...] = jnp.zeros(a_ref.shape, F32)
        car[...] = jnp.zeros(car.shape, F32)
        step((mf, lf, af), qbf[...], [knf_ref[...]], bnf_ref[...], [vnf_ref[...]])
        step((md, ld, ad), qbd[...], [knd_ref[...]], expand_rows(seln_ref[...]) + t5n_ref[...], [vnd_ref[...]])

    carry = car[...]
    cn = cn_ref[...]
    bias = [None] * ppc
    for j in reversed(range(ppc)):
        wt = wt_p[j][...]
        suf = wt[0:8] + carry
        bias[j] = jnp.concatenate([suf] * t_new, axis=0) + cn
        carry = carry + wt[8:16]
    car[...] = carry
    step((mf, lf, af), qbf[...], [r[...] for r in kf_p], jnp.concatenate(bias, axis=1), [r[...] for r in vf_p])
    step((md, ld, ad), qbd[...], [r[...] for r in kd_p], expand_rows(selb_ref[...]) + t5_ref[nc - 1 - c],
         [r[...] for r in vd_p])

    @pl.when(c == nc - 1)
    def _():
        hm = jnp.concatenate([jnp.where(headmask, 1.0, 0.0)] * t_new, axis=0)
        for (l_ref, a_ref), o_ref in (((lf, af), of_ref), ((ld, ad), od_ref)):
            o = (a_ref[...] / l_ref[...]) * hm
            o_ref[...] = jnp.concatenate([jnp.sum(o[8 * t:8 * t + 8], axis=0, keepdims=True) for t in range(t_new)], axis=0)


def _sample_attention(page_table, layer, qf, qd, knf, vnf, knd, vnd, cn32, bnf, selb, seln, t5c, t5n,
                      cache_fox_k, cache_fox_v, cache_dsa_k, cache_dsa_v, wt, *, nh, dh):
    dbsz, n_pages = page_table.shape
    t_new = qf.shape[1]
    wf = nh * dh
    page = cache_fox_k.shape[3]
    ppc = 8
    while n_pages % ppc:
        ppc //= 2
    nc = n_pages // ppc
    rows = t_new * 8
    cwid = ppc * page
    assert page == LANES and nh == 8

    def pg(j):
        return lambda b, c, pt: (layer, pt[b, (nc - 1 - c) * ppc + j], 0, 0)

    def wtm(j):
        return lambda b, c, pt: (pt[b, (nc - 1 - c) * ppc + j], 0, 0)

    kv_specs = [pl.BlockSpec((None, None, wf, page), pg(j)) for j in range(ppc)]
    wt_specs = [pl.BlockSpec((None, 16, LANES), wtm(j)) for j in range(ppc)]
    per_b3 = lambda shp: pl.BlockSpec((None,) + shp, lambda b, c, pt: (b, 0, 0))
    caches = lambda a: [a] * ppc
    return pl.pallas_call(
        functools.partial(_sattn_kernel, ppc=ppc, nc=nc, t_new=t_new, nh=nh, dh=dh),
        grid_spec=pltpu.PrefetchScalarGridSpec(
            num_scalar_prefetch=1,
            grid=(dbsz, nc),
            in_specs=[per_b3((t_new, wf)), per_b3((t_new, wf)),
                      per_b3((wf, page)), per_b3((wf, page)), per_b3((wf, page)), per_b3((wf, page)),
                      per_b3((rows, LANES)), per_b3((rows, LANES)),
                      pl.BlockSpec((None, None, t_new, cwid), lambda b, c, pt: (b, nc - 1 - c, 0, 0)),
                      per_b3((t_new, LANES)),
                      _resident(t5c.shape, lambda b, c, pt: (0, 0, 0)),
                      _resident(t5n.shape, lambda b, c, pt: (0, 0))]
                     + kv_specs * 4 + wt_specs,
            out_specs=[per_b3((t_new, wf)), per_b3((t_new, wf))],
            scratch_shapes=[pltpu.VMEM((rows, wf), BF16), pltpu.VMEM((rows, wf), BF16),
                            pltpu.VMEM((rows, 1), F32), pltpu.VMEM((rows, 1), F32), pltpu.VMEM((rows, wf), F32),
                            pltpu.VMEM((rows, 1), F32), pltpu.VMEM((rows, 1), F32), pltpu.VMEM((rows, wf), F32),
                            pltpu.VMEM((8, LANES), F32)],
        ),
        out_shape=[jax.ShapeDtypeStruct((dbsz, t_new, wf), F32)] * 2,
        compiler_params=_cparams(("arbitrary", "arbitrary")),
        name="sample_attention",
    )(page_table, qf, qd, knf, vnf, knd, vnd, cn32, bnf, selb, seln, t5c, t5n,
      *caches(cache_fox_k), *caches(cache_fox_v), *caches(cache_dsa_k), *caches(cache_dsa_v), *([wt] * ppc))


def kernel(x_prompt, x_sample, cache_fox_k, cache_fox_v, cache_fox_logf, cache_dsa_k, cache_dsa_v, cache_idx_k,
           page_table, c_prompt, c_sample, g_pre, w_mod, b_mod, w_in, b_forget, rel_bias,
           w_fox_out, w_dsa_out, w_out, g_post):
    bsz, seq, d = x_prompt.shape
    dbsz, t_new, _ = x_sample.shape
    depth, n_phys, page, nh, dh = cache_fox_k.shape
    di = cache_idx_k.shape[-1]
    kv_t = lambda a: a.transpose(0, 1, 3, 4, 2).reshape(depth, n_phys, nh * dh, page)
    cache_fox_k, cache_fox_v, cache_dsa_k, cache_dsa_v = map(kv_t, (cache_fox_k, cache_fox_v, cache_dsa_k, cache_dsa_v))
    cache_idx_k = cache_idx_k.transpose(0, 1, 3, 2)
    wf = nh * dh
    nhi = w_in.shape[-1] - (8 * wf + nh + di + 2 * d)
    nhi = nhi // (di + 1)
    n_pages = page_table.shape[1]
    past = n_pages * page
    n_buckets = rel_bias.shape[0]
    attn_scale = dh ** -0.5
    idx_w_scale = (nhi * di) ** -0.5
    geom = dict(nh=nh, dh=dh, nhi=nhi, di=di, attn_scale=attn_scale, idx_w_scale=idx_w_scale)
    assert nh == 8 and nhi == 8 and page == LANES

    new_p = [[] for _ in range(6)]
    new_s = [[] for _ in range(6)]
    h_p, h_s = x_prompt, x_sample
    n_c = bsz + dbsz
    n_c_pad = -(-n_c // 8) * 8
    c_all = jnp.pad(jnp.concatenate([c_prompt, c_sample], axis=0), ((0, n_c_pad - n_c), (0, 0)))
    topk_s = min(TOPK_MAX, (past + t_new) // 4)
    lpad = past + 4 * LANES

    for l in range(depth):
        mod = _mod(c_all, w_mod[l], b_mod[l])
        shift, scale, gate = mod[:, :d], mod[:, d:2 * d], mod[:, 2 * d:]
        rep = lambda a: jnp.repeat(a[bsz:n_c], t_new, axis=0)

        x2 = h_p.reshape(bsz * seq, d)
        (fk, fv, lf, dk, dv, ik, fz, dz, gf, gd, iw, qa, ka, fv_h, dq_h, dk_h, dv_h, iq_h, ik_b) = _proj(
            x2, scale[:bsz, None, :], shift[:bsz, None, :], g_pre[l], w_in[l], b_forget[l],
            prompt=True, bsz=bsz, seq=seq, **geom)
        o_fox = _fox_prompt(qa, ka, fv_h, dh)
        o_dsa = _dsa_prompt(iq_h, iw, ik_b, dq_h, dk_h, dv_h, rel_bias)
        y2 = _mix_out(x2, gate[:bsz, None, :], o_fox, fz, o_dsa, dz, gf, gd, w_fox_out[l], w_dsa_out[l], w_out[l],
                      g_post[l], head_major=True, seq=seq, nh=nh, dh=dh)
        h_p = y2.reshape(bsz, seq, d)
        for lst, a, shp in zip(new_p, (fk, fv, lf, dk, dv, ik),
                               ((nh, dh), (nh, dh), (nh,), (nh, dh), (nh, dh), (di,))):
            lst.append(a.reshape((bsz, seq) + shp))

        xs2 = h_s.reshape(dbsz * t_new, d)
        (sfk, sfv, slf, sdk, sdv, sik, sfz, sdz, sgf, sgd, siw, sfq, sdq, siq, scn) = _proj(
            xs2, rep(scale), rep(shift), g_pre[l], w_in[l], b_forget[l],
            prompt=False, bsz=dbsz, seq=t_new, **geom)
        q32 = siq.reshape(dbsz, t_new * nhi, di)
        w32 = jnp.broadcast_to(siw.reshape(dbsz, t_new * nhi, 1), (dbsz, t_new * nhi, LANES))
        pad_new = lambda a: jnp.pad(a.reshape(dbsz, t_new, -1).transpose(0, 2, 1), ((0, 0), (0, 0), (0, page - t_new)))
        scores = _sample_idx_scores(page_table, q32, w32, pad_new(sik), cache_idx_k, l, lpad)
        sel = _sample_select(scores.reshape(dbsz * t_new, lpad), t_new, past, topk_s).reshape(dbsz, t_new, lpad)
        tpos = jnp.arange(t_new)
        rel = past + tpos[:, None] - jnp.arange(past + page)[None, :]
        t5 = rel_bias[_t5_bucket(rel, n_buckets)].astype(F32)
        t5 = t5.transpose(0, 2, 1).reshape(t_new * nh, past + page)
        ppc = 8
        while n_pages % ppc:
            ppc //= 2
        nc = n_pages // ppc
        t5c = t5[:, :past].reshape(t_new * nh, nc, ppc * page).transpose(1, 0, 2)
        t5n = t5[:, past:]
        selb = sel[:, :, :past].reshape(dbsz, t_new, nc, ppc * page).transpose(0, 2, 1, 3)
        seln = sel[:, :, past:past + page]
        cn = scn[:, :nh].reshape(dbsz, t_new, nh)
        cn32 = jnp.broadcast_to(cn.reshape(dbsz, t_new * nh, 1), (dbsz, t_new * nh, LANES))
        dcn = cn[:, :, None, :] - cn[:, None, :, :]
        dcn = jnp.where((tpos[None, :] <= tpos[:, None])[None, :, :, None], dcn, NEG_INF)
        bnf = jnp.pad(dcn.transpose(0, 1, 3, 2).reshape(dbsz, t_new * nh, t_new), ((0, 0), (0, 0), (0, LANES - t_new)),
                      constant_values=NEG_INF)
        wt = _page_suffix(cache_fox_logf[l].transpose(0, 2, 1))
        so_fox, so_dsa = _sample_attention(
            page_table, l, sfq.reshape(dbsz, t_new, wf), sdq.reshape(dbsz, t_new, wf),
            pad_new(sfk), pad_new(sfv), pad_new(sdk), pad_new(sdv), cn32, bnf, selb, seln, t5c, t5n,
            cache_fox_k, cache_fox_v, cache_dsa_k, cache_dsa_v, wt, nh=nh, dh=dh)
        ys2 = _mix_out(xs2, rep(gate), so_fox.reshape(dbsz * t_new, wf), sfz, so_dsa.reshape(dbsz * t_new, wf), sdz,
                       sgf, sgd, w_fox_out[l], w_dsa_out[l], w_out[l], g_post[l], head_major=False, seq=t_new,
                       nh=nh, dh=dh)
        h_s = ys2.reshape(dbsz, t_new, d)
        for lst, a, shp in zip(new_s, (sfk, sfv, slf, sdk, sdv, sik),
                               ((nh, dh), (nh, dh), (nh,), (nh, dh), (nh, dh), (di,))):
            lst.append(a.reshape((dbsz, t_new) + shp))

    return (h_p, h_s) + tuple(jnp.stack(v) for v in new_p) + tuple(jnp.stack(v) for v in new_s)
```

```python
import functools

import numpy as np
import jax
import jax.numpy as jnp
from jax import lax
from jax.experimental import pallas as pl
from jax.experimental.pallas import tpu as pltpu

F32 = jnp.float32
BF16 = jnp.bfloat16
I32 = jnp.int32

LANES = 128
TOPK_MAX = 256
MAX_DISTANCE = 128
RMS_EPS = 1e-6
NEG_INF = float("-inf")
M_INIT = -1e30
LOG2E = 1.4426950408889634
INT_MIN = -(2 ** 31)
INT_MAX = 2 ** 31 - 1
N_AUG = 3
VMEM_LIMIT = 60 * 1024 * 1024
HEADS_PER_STEP = 2
SCORE_TILE = 4 * LANES
DSA_CHUNK = 8 * LANES
PAGES_PER_STEP = 16

NT = (((1,), (1,)), ((), ()))


def _cparams(sem):
    return pltpu.CompilerParams(dimension_semantics=sem, vmem_limit_bytes=VMEM_LIMIT)


def _resident(block_shape, index_map):
    return pl.BlockSpec(block_shape, index_map, pipeline_mode=pl.Buffered(1))


def _split3(x):
    hi = x.astype(BF16)
    r1 = x - hi.astype(F32)
    mid = r1.astype(BF16)
    lo = (r1 - mid.astype(F32)).astype(BF16)
    return hi, mid, lo


def _dot(a, b):
    return jnp.dot(a, b, preferred_element_type=F32)


def _dot_nt(a, b):
    return lax.dot_general(a, b, NT, preferred_element_type=F32)


def _sigmoid(x):
    return 1.0 / (1.0 + jnp.exp(-x))


def _silu(x):
    return x * _sigmoid(x)


def _mod_kernel(c_ref, w_ref, b_ref, o_ref):
    a = _silu(c_ref[...])
    a_hi, a_mid, _ = _split3(a)
    w_hi, w_mid, _ = _split3(w_ref[...])
    o_ref[...] = _dot(a_hi, w_hi) + (_dot(a_hi, w_mid) + _dot(a_mid, w_hi)) + b_ref[...]


def _mod(c, w_mod, b_mod):
    n, d = c.shape
    e = w_mod.shape[1]
    te = 512
    assert e % te == 0 and n % 8 == 0
    return pl.pallas_call(
        _mod_kernel,
        grid=(e // te,),
        in_specs=[pl.BlockSpec((n, d), lambda j: (0, 0)),
                  pl.BlockSpec((d, te), lambda j: (0, j)),
                  pl.BlockSpec((1, te), lambda j: (0, j))],
        out_specs=pl.BlockSpec((n, te), lambda j: (0, j)),
        out_shape=jax.ShapeDtypeStruct((n, e), F32),
        compiler_params=_cparams(("arbitrary",)),
        name="mod",
    )(c, w_mod, b_mod.reshape(1, e))


def _proj_kernel(x_ref, sc_ref, sh_ref, g_ref, w_ref, bf_ref, pl_ref, *refs,
                 cols, prompt, group, tm, nh, dh, nhi, di, idx_w_scale):
    if prompt:
        (fk_o, fv_o, lf_o, dk_o, dv_o, ik_o, fz_o, dz_o, gf_o, gd_o, iw_o,
         qa_o, ka_o, fvh_o, dqh_o, dkc_o, dvh_o, iqh_o, ikc_o, carry) = refs
    else:
        (fk_o, fv_o, lf_o, dk_o, dv_o, ik_o, fz_o, dz_o, gf_o, gd_o, iw_o,
         fq_o, dq_o, iq_o, cn_o, carry) = refs
    i = pl.program_id(0)
    x = x_ref[...]
    y = x * lax.rsqrt(jnp.mean(x * x, axis=-1, keepdims=True) + RMS_EPS) * g_ref[...]
    hb = (y * (1.0 + sc_ref[...]) + sh_ref[...]).astype(BF16)

    def mm(name):
        c0, c1 = cols[name]
        return _dot(hb, w_ref[:, c0:c1])

    fk = mm("fk"); fv = mm("fv"); dk = mm("dk"); dv = mm("dv")
    fz_o[...] = mm("fz")
    dz_o[...] = mm("dz")
    gf_o[...] = mm("gf")
    gd_o[...] = mm("gd")
    small = mm("small")
    iw_o[...] = small[:, 8:8 + nhi] * idx_w_scale
    z = small + bf_ref[...]
    logf_full = jnp.minimum(z, 0.0) - jnp.log(1.0 + jnp.exp(-jnp.abs(z)))
    lane = lax.broadcasted_iota(I32, (tm, LANES), 1)
    logf = jnp.where(lane < nh, logf_full, 0.0)
    if prompt:
        fk_o[...] = fk.T
        fv_o[...] = fv.T
        dkt = dk.T
        dk_o[...] = dkt
        dkc_o[...] = dkt.astype(BF16)
        dv_o[...] = dv.T
        lf_o[...] = logf.T[:nh]
        ikt = small.T[LANES - di:]
        ik_o[...] = ikt
        ikc_o[...] = ikt.astype(BF16)
    else:
        fk_o[...] = fk
        fv_o[...] = fv
        dk_o[...] = dk
        dv_o[...] = dv
        lf_o[...] = logf[:, :nh]
        ik_o[...] = small[:, LANES - di:]

    r = lax.broadcasted_iota(I32, (tm, tm), 0)
    c = lax.broadcasted_iota(I32, (tm, tm), 1)
    g_in = min(group, tm)
    tri = jnp.where((c <= r) & ((c // g_in) == (r // g_in)), 1.0, 0.0).astype(BF16)
    l_hi, l_mid, l_lo = _split3(logf)
    cum = _dot(tri, l_lo) + _dot(tri, l_mid) + _dot(tri, l_hi)
    if group > tm:
        @pl.when((i * tm) % group == 0)
        def _():
            carry[...] = jnp.zeros_like(carry)
        cum = cum + carry[...]
        carry[...] = cum[tm - 1:tm, :]

    if prompt:
        c_hi, c_mid, c_lo = _split3(cum * LOG2E)
        aug = _dot(c_hi, pl_ref[0]) + _dot(c_mid, pl_ref[1]) + _dot(c_lo, pl_ref[2]) + pl_ref[3][0:1, :].astype(F32)
        wa = nh * LANES
        qa = mm("qa") + aug[:, :wa]
        ka = mm("ka") + aug[:, wa:]
        dq = mm("dq"); iq = mm("iq")
        ones_pad = jnp.where(lax.broadcasted_iota(I32, (tm, LANES - dh), 1) == 0, 1.0, 0.0)
        for h in range(nh):
            qa_o[h] = qa[:, h * LANES:(h + 1) * LANES].astype(BF16)
            ka_o[h] = ka[:, h * LANES:(h + 1) * LANES].astype(BF16)
            fvh_o[h] = jnp.concatenate([fv[:, h * dh:(h + 1) * dh], ones_pad], axis=1).astype(BF16)
            dqh_o[h] = dq[:, h * dh:(h + 1) * dh].astype(BF16)
            dvh_o[h] = jnp.concatenate([dv[:, h * dh:(h + 1) * dh], ones_pad], axis=1).astype(BF16)
        for h in range(nhi):
            iqh_o[h] = iq[:, h * di:(h + 1) * di].astype(BF16)
    else:
        fq_o[...] = mm("fq").astype(BF16)
        dq_o[...] = mm("dq").astype(BF16)
        iq_o[...] = mm("iq").astype(BF16)
        cn_o[...] = cum


def _proj_weights(w_in, b_forget, prompt, nh, dh, nhi, di, d, attn_scale):
    wf = nh * dh
    sizes = [wf, wf, wf, nh, wf, wf, wf, wf, nhi * di, di, nhi, wf, d, d]
    offs = np.cumsum([0] + sizes)
    fq, fk, fv, ff, fz, dq, dk, dv, iq, ik, iw, dz, gf, gd = [w_in[:, offs[j]:offs[j + 1]] for j in range(14)]
    fq = fq * (attn_scale * (LOG2E if prompt else 1.0))
    dq = dq * (attn_scale * (LOG2E if prompt else 1.0))
    small = jnp.zeros((d, LANES), F32)
    small = small.at[:, 0:nh].set(ff).at[:, 8:8 + nhi].set(iw).at[:, LANES - di:].set(ik)

    def pad_heads(w):
        return jnp.pad(w.reshape(d, nh, dh), ((0, 0), (0, 0), (0, LANES - dh))).reshape(d, nh * LANES)

    if prompt:
        groups = [("qa", pad_heads(fq)), ("ka", pad_heads(fk))]
    else:
        groups = [("fq", fq)]
    groups += [("fk", fk), ("fv", fv), ("fz", fz), ("dq", dq), ("dk", dk), ("dv", dv), ("iq", iq), ("dz", dz),
               ("gf", gf), ("gd", gd), ("small", small)]
    cols, c0 = {}, 0
    for name, w in groups:
        cols[name] = (c0, c0 + w.shape[1])
        c0 += w.shape[1]
    w_all = jnp.concatenate([w for _, w in groups], axis=1).astype(BF16)
    bf_row = jnp.zeros((1, LANES), F32).at[0, 0:nh].set(b_forget)
    return w_all, cols, bf_row


def _placement(nh, dh):
    wa = nh * LANES
    p = np.zeros((N_AUG + 1, LANES, 2 * wa), np.float32)
    for h in range(nh):
        for c in range(N_AUG):
            p[c, h, h * LANES + dh + c] = 1.0
            p[c, h, wa + h * LANES + dh + N_AUG + c] = -1.0
            p[N_AUG, :, h * LANES + dh + N_AUG + c] = 1.0
            p[N_AUG, :, wa + h * LANES + dh + c] = 1.0
    return jnp.asarray(p, BF16)


def _proj(x2, scale, shift, g_pre, w_in, b_forget, *, prompt, bsz, seq, nh, dh, nhi, di, attn_scale, idx_w_scale):
    n, d = x2.shape
    wf = nh * dh
    assert dh + 2 * N_AUG <= LANES and nh <= 8 and nhi <= 8 and di <= LANES - 16
    w_all, cols, bf_row = _proj_weights(w_in, b_forget, prompt, nh, dh, nhi, di, d, attn_scale)
    place = _placement(nh, dh)
    tm = 256 if prompt else min(n, 256)
    assert n % tm == 0 and seq % min(seq, tm) == 0
    nt = seq // tm if prompt else 1
    row = lambda i: (i, 0)
    if prompt:
        mod_spec = pl.BlockSpec((None, 1, d), lambda i: (i // nt, 0, 0))
        hm = lambda i: (i // nt, 0, i % nt, 0)
    else:
        mod_spec = pl.BlockSpec((tm, d), row)
    f32o = lambda w: (jax.ShapeDtypeStruct((n, w), F32), pl.BlockSpec((tm, w), row))
    if prompt:
        f32t = lambda r: (jax.ShapeDtypeStruct((bsz, r, seq), F32), pl.BlockSpec((None, r, tm), lambda i: (i // nt, 0, i % nt)))
        outs = [f32t(wf), f32t(wf), f32t(nh), f32t(wf), f32t(wf), f32t(di)]
    else:
        outs = [f32o(wf), f32o(wf), f32o(nh), f32o(wf), f32o(wf), f32o(di)]
    outs += [f32o(wf), f32o(wf), f32o(d), f32o(d), f32o(nhi)]
    if prompt:
        hm_o = lambda heads, w: (jax.ShapeDtypeStruct((bsz, heads, seq, w), BF16), pl.BlockSpec((None, heads, tm, w), hm))

        def tiled_t(r, width):
            assert seq % width == 0 and width % tm == 0
            per = width // tm
            return (jax.ShapeDtypeStruct((bsz, seq // width, r, width), BF16),
                    pl.BlockSpec((None, None, r, tm), lambda i: (i // nt, (i % nt) // per, 0, (i % nt) % per)))

        outs += [hm_o(nh, LANES), hm_o(nh, LANES), hm_o(nh, LANES), hm_o(nh, dh), tiled_t(wf, DSA_CHUNK), hm_o(nh, LANES),
                 hm_o(nhi, di), tiled_t(di, SCORE_TILE)]
    else:
        b16o = lambda w: (jax.ShapeDtypeStruct((n, w), BF16), pl.BlockSpec((tm, w), row))
        outs += [b16o(wf), b16o(wf), b16o(nhi * di), f32o(LANES)]
    kern = functools.partial(_proj_kernel, cols=cols, prompt=prompt, group=seq, tm=tm, nh=nh, dh=dh, nhi=nhi, di=di,
                             idx_w_scale=idx_w_scale)
    return pl.pallas_call(
        kern,
        grid=(n // tm,),
        in_specs=[pl.BlockSpec((tm, d), row), mod_spec, mod_spec,
                  _resident((1, d), lambda i: (0, 0)),
                  _resident(w_all.shape, lambda i: (0, 0)),
                  _resident((1, LANES), lambda i: (0, 0)),
                  _resident(place.shape, lambda i: (0, 0, 0))],
        out_specs=[o[1] for o in outs],
        out_shape=[o[0] for o in outs],
        scratch_shapes=[pltpu.VMEM((1, LANES), F32)],
        compiler_params=_cparams(("arbitrary",)),
        name="proj_prompt" if prompt else "proj_sample",
    )(x2, scale, shift, g_pre.reshape(1, d), w_all, bf_row, place)


def _attend(carry, s, v):
    m, l, acc = carry
    m_new = jnp.maximum(m, jnp.max(s, axis=1, keepdims=True))
    alpha = jnp.exp(m - m_new)
    p = jnp.exp(s - m_new)
    l = alpha * l + jnp.sum(p, axis=1, keepdims=True)
    acc = alpha * acc + _dot(p.astype(BF16), v)
    return m_new, l, acc


def _attend_init(rows, dv):
    return (jnp.full((rows, 1), M_INIT, F32), jnp.zeros((rows, 1), F32), jnp.zeros((rows, dv), F32))


def _fold_lanes(x, op):
    out = x[:, :LANES]
    for j in range(1, x.shape[1] // LANES):
        out = op(out, x[:, j * LANES:(j + 1) * LANES])
    return out


def _fox_kernel(q_ref, k_ref, v_ref, o_ref, s_ref, mx_ref, acc_ref, *, tq, cw):
    qi = pl.program_id(2)
    nfull = (qi * tq) // cw
    hs = range(HEADS_PER_STEP)
    q = [q_ref[g] for g in hs]

    def logits(g, c):
        return _dot_nt(q[g], k_ref[g, pl.ds(pl.multiple_of(c * cw, cw), cw), :])

    mx_ref[...] = jnp.full(mx_ref.shape, M_INIT, F32)

    def p1(c, _):
        for g in hs:
            s = logits(g, c)
            s_ref[g, c] = s
            mx_ref[g] = jnp.maximum(mx_ref[g], _fold_lanes(s, jnp.maximum))
        return 0

    lax.fori_loop(0, nfull, p1, 0)
    row = qi * tq + lax.broadcasted_iota(I32, (tq, cw), 0)
    col = nfull * cw + lax.broadcasted_iota(I32, (tq, cw), 1)
    m = []
    for g in hs:
        s = jnp.where(col <= row, logits(g, nfull), NEG_INF)
        s_ref[g, nfull] = s
        m.append(jnp.max(jnp.maximum(mx_ref[g], _fold_lanes(s, jnp.maximum)), axis=1, keepdims=True))

    acc_ref[...] = jnp.zeros(acc_ref.shape, F32)

    def p2(c, _):
        for g in hs:
            p = jnp.exp2(s_ref[g, c] - m[g])
            acc_ref[g] += _dot(p.astype(BF16), v_ref[g, pl.ds(pl.multiple_of(c * cw, cw), cw), :])
        return 0

    lax.fori_loop(0, nfull + 1, p2, 0)
    dv = o_ref.shape[-1]
    for g in hs:
        acc = acc_ref[g]
        o_ref[g] = acc[:, :dv] / acc[:, dv:dv + 1]


def _fox_prompt(qa, ka, fv1, dv):
    bsz, nh, seq, wa = qa.shape
    wv = fv1.shape[-1]
    tq = min(256, seq)
    cw = min(1024, seq)
    g = HEADS_PER_STEP
    assert seq % cw == 0 and cw % tq == 0 and nh % g == 0 and dv < wv
    return pl.pallas_call(
        functools.partial(_fox_kernel, tq=tq, cw=cw),
        grid=(bsz, nh // g, seq // tq),
        in_specs=[pl.BlockSpec((None, g, tq, wa), lambda b, h, i: (b, h, i, 0)),
                  pl.BlockSpec((None, g, seq, wa), lambda b, h, i: (b, h, 0, 0)),
                  pl.BlockSpec((None, g, seq, wv), lambda b, h, i: (b, h, 0, 0))],
        out_specs=pl.BlockSpec((None, g, tq, dv), lambda b, h, i: (b, h, i, 0)),
        out_shape=jax.ShapeDtypeStruct((bsz, nh, seq, dv), F32),
        scratch_shapes=[pltpu.VMEM((g, seq // cw, tq, cw), F32), pltpu.VMEM((g, tq, LANES), F32),
                        pltpu.VMEM((g, tq, wv), F32)],
        compiler_params=_cparams(("arbitrary", "arbitrary", "arbitrary")),
        name="fox_prompt",
    )(qa, ka, fv1)


def _sort_key(x):
    bits = lax.bitcast_convert_type(x, I32)
    return jnp.where(bits < 0, bits ^ jnp.int32(INT_MAX), bits)


def _select_topk(sc_ref, nslab4, rows, kk, idx_bits, row0=0):
    lane = lax.broadcasted_iota(I32, (rows, LANES), 1)
    kf = jnp.float32(kk)
    rsl = slice(row0, row0 + rows)
    ngrp = -(-kk // LANES)
    assert 4 % ngrp == 0

    def count(pred):
        def body(c4, cnt):
            for j in range(4):
                s = c4 * 4 + j
                cnt = cnt + jnp.where(pred(sc_ref[s, rsl, :], s), 1.0, 0.0)
            return cnt
        cnt = lax.fori_loop(0, nslab4, body, jnp.zeros((rows, LANES), F32))
        return jnp.sum(cnt, axis=1, keepdims=True)

    def gmax(c4, g):
        g = list(g)
        for j in range(4):
            g[j % ngrp] = jnp.maximum(g[j % ngrp], sc_ref[c4 * 4 + j, rsl, :])
        return tuple(g)

    g = lax.fori_loop(0, nslab4, gmax, tuple(jnp.full((rows, LANES), INT_MIN, I32) for _ in range(ngrp)))
    g_lo, g_hi = g[0], g[0]
    for x in g[1:]:
        g_lo, g_hi = jnp.minimum(g_lo, x), jnp.maximum(g_hi, x)
    lo = jnp.min(g_lo, axis=1, keepdims=True)
    hi = jnp.max(g_hi, axis=1, keepdims=True)

    width = 32 - lax.clz(hi - lo)
    nsteps = jnp.max(width.astype(F32)).astype(I32)

    def bisect(_, lohi):
        lo, hi = lohi
        d = hi - lo
        mid = lo + lax.shift_right_logical(d, 1) + (d & 1)
        ok = count(lambda k, s: k >= mid) >= kf
        live = d != 0
        return jnp.where(live & ok, mid, lo), jnp.where(live & ~ok, mid - 1, hi)

    thr, _ = lax.fori_loop(0, nsteps, bisect, (lo, hi))
    thr = jnp.maximum(thr, INT_MIN + 1)
    c_gt = count(lambda k, s: k > thr)
    c_ge = count(lambda k, s: k >= thr)
    need = kf - c_gt

    def idx_step(i, cut):
        cand = cut + jnp.left_shift(jnp.int32(1), idx_bits - 1 - i)
        c = count(lambda k, s: (k == thr) & ((s * LANES + lane) < cand))
        return jnp.where(c < need, cand, cut)

    surplus = c_ge > kf
    cut = lax.cond(jnp.max(jnp.where(surplus, 1.0, 0.0)) > 0.0,
                   lambda: lax.fori_loop(0, idx_bits, idx_step, jnp.zeros((rows, 1), I32)),
                   lambda: jnp.zeros((rows, 1), I32))
    cut = jnp.where(surplus, cut, INT_MAX)

    def fin(c4, _):
        for j in range(4):
            s = c4 * 4 + j
            k = sc_ref[s, rsl, :]
            sel = (k > thr) | ((k == thr) & ((s * LANES + lane) <= cut))
            sc_ref[s, rsl, :] = lax.bitcast_convert_type(jnp.where(sel, 0.0, NEG_INF), I32)
        return 0

    lax.fori_loop(0, nslab4, fin, 0)


def _mask_slab(sc_ref, s):
    return lax.bitcast_convert_type(sc_ref[s], F32)


def _dsa_kernel(iq_ref, iw_ref, ik_ref, dq_ref, dk_ref, dv_ref, tab_ref, o_ref, sc_ref, s_ref, mx_ref, acc_ref,
                *, tq, topk, nh, nhi, idx_bits, spc):
    qi = pl.program_id(1)
    rb = tq // LANES
    nch = (qi * rb + rb - 1) // spc + 1
    cw = spc * LANES
    sw = SCORE_TILE
    w = iw_ref[...]
    row = qi * tq + lax.broadcasted_iota(I32, (LANES, sw), 0)

    def scores(c, _):
        kt = ik_ref[c]
        col = c * sw + lax.broadcasted_iota(I32, (LANES, sw), 1)
        for r in range(rb):
            rows = slice(r * LANES, (r + 1) * LANES)
            acc = jnp.zeros((LANES, sw), F32)
            for h in range(nhi):
                acc = acc + jnp.maximum(_dot(iq_ref[h, rows, :], kt), 0.0) * w[rows, h:h + 1]
            key = jnp.where(col <= row + r * LANES, _sort_key(acc), INT_MIN)
            for j in range(sw // LANES):
                sc_ref[c * (sw // LANES) + j, rows, :] = key[:, j * LANES:(j + 1) * LANES]
        return 0

    lax.fori_loop(0, nch * (cw // sw), scores, 0)
    for r in range(rb):
        _select_topk(sc_ref, nch * (spc // 4), LANES, topk, idx_bits, row0=r * LANES)

    hs = range(HEADS_PER_STEP)
    dv = o_ref.shape[-1]

    def heads(hp, _):
        h0 = hp * HEADS_PER_STEP
        q = [dq_ref[h0 + g] for g in hs]
        mx_ref[...] = jnp.full(mx_ref.shape, M_INIT, F32)

        def p1(c, _):
            mask = [_mask_slab(sc_ref, c * spc + j) for j in range(spc)]
            tsel = [[jnp.clip(qi * rb + r - (c * spc + j), 0, 2) for r in range(rb)] for j in range(spc)]
            for g in hs:
                bias = jnp.concatenate(
                    [mask[j] + jnp.concatenate([tab_ref[tsel[j][r], h0 + g] for r in range(rb)], axis=0)
                     for j in range(spc)], axis=1)
                kt = dk_ref[c, h0 + g]
                s = _dot(q[g], kt) + bias
                s_ref[g, c] = s
                mx_ref[g] = jnp.maximum(mx_ref[g], _fold_lanes(s, jnp.maximum))
            return 0

        lax.fori_loop(0, nch, p1, 0)
        m = [jnp.max(mx_ref[g], axis=1, keepdims=True) for g in hs]
        acc_ref[...] = jnp.zeros(acc_ref.shape, F32)

        def p2(c, _):
            ks = pl.ds(pl.multiple_of(c * cw, cw), cw)
            for g in hs:
                p = jnp.exp2(s_ref[g, c] - m[g])
                acc_ref[g] += _dot(p.astype(BF16), dv_ref[h0 + g, ks, :])
            return 0

        lax.fori_loop(0, nch, p2, 0)
        for g in hs:
            acc = acc_ref[g]
            o_ref[h0 + g] = acc[:, :dv] / acc[:, dv:dv + 1]
        return 0

    lax.fori_loop(0, nh // HEADS_PER_STEP, heads, 0)


def _t5_bucket(rel, n_buckets):
    max_exact = n_buckets // 2
    n = jnp.maximum(rel, 0)
    nf = jnp.maximum(n, 1).astype(F32)
    large = max_exact + (jnp.log(nf / max_exact) / np.log(MAX_DISTANCE / max_exact) * (n_buckets - max_exact)).astype(I32)
    return jnp.where(n < max_exact, n, jnp.minimum(large, n_buckets - 1))


def _dsa_prompt(iq_h, iw, ik_c, dq_h, dk_c, dv1_h, rel_bias):
    bsz, nh, seq, dh = dq_h.shape
    nhi, di = iq_h.shape[1], iq_h.shape[3]
    wv = dv1_h.shape[-1]
    n_buckets = rel_bias.shape[0]
    tq = min(2 * LANES, seq)
    spc = DSA_CHUNK // LANES
    assert seq % DSA_CHUNK == 0 and DSA_CHUNK % SCORE_TILE == 0 and LANES == MAX_DISTANCE and dh < wv
    topk = min(TOPK_MAX, seq // 4)
    i = jnp.arange(LANES)
    rel = jnp.arange(2)[:, None, None] * LANES + i[None, :, None] - i[None, None, :]
    far = rel_bias[n_buckets - 1].astype(F32)
    tab = rel_bias[_t5_bucket(rel, n_buckets)].astype(F32) - far
    tab = jnp.concatenate([tab.transpose(0, 3, 1, 2) * LOG2E, jnp.zeros((1, nh, LANES, LANES), F32)], axis=0)
    nslab = seq // LANES
    kern = functools.partial(_dsa_kernel, tq=tq, topk=topk, nh=nh, nhi=nhi, idx_bits=max(1, (seq - 1).bit_length()),
                             spc=spc)
    return pl.pallas_call(
        kern,
        grid=(bsz, seq // tq),
        in_specs=[pl.BlockSpec((None, nhi, tq, di), lambda b, i: (b, 0, i, 0)),
                  pl.BlockSpec((None, tq, nhi), lambda b, i: (b, i, 0)),
                  _resident((None,) + ik_c.shape[1:], lambda b, i: (b, 0, 0, 0)),
                  pl.BlockSpec((None, nh, tq, dh), lambda b, i: (b, 0, i, 0)),
                  _resident((None,) + dk_c.shape[1:], lambda b, i: (b, 0, 0, 0, 0)),
                  _resident((None, nh, seq, wv), lambda b, i: (b, 0, 0, 0)),
                  _resident(tab.shape, lambda b, i: (0, 0, 0, 0))],
        out_specs=pl.BlockSpec((None, nh, tq, dh), lambda b, i: (b, 0, i, 0)),
        out_shape=jax.ShapeDtypeStruct((bsz, nh, seq, dh), F32),
        scratch_shapes=[pltpu.VMEM((nslab, tq, LANES), I32),
                        pltpu.VMEM((HEADS_PER_STEP, nslab // spc, tq, spc * LANES), F32),
                        pltpu.VMEM((HEADS_PER_STEP, tq, LANES), F32), pltpu.VMEM((HEADS_PER_STEP, tq, wv), F32)],
        compiler_params=_cparams(("arbitrary", "arbitrary")),
        name="dsa_prompt",
    )(iq_h, iw.reshape(bsz, seq, nhi), ik_c, dq_h, dk_c, dv1_h, tab)


def _out_kernel(x_ref, gate_ref, of_ref, fz_ref, od_ref, dz_ref, gf_ref, gd_ref, wf_ref, wd_ref, wo_ref, gp_ref, y_ref,
                *, head_major, nh, dh):
    def branch(o_ref, z_ref, w_ref):
        g = _silu(z_ref[...])
        if head_major:
            a = None
            for h in range(nh):
                t = _dot((o_ref[h] * g[:, h * dh:(h + 1) * dh]).astype(BF16), w_ref[h * dh:(h + 1) * dh, :])
                a = t if a is None else a + t
            return a
        return _dot((o_ref[...] * g).astype(BF16), w_ref[...])

    a = branch(of_ref, fz_ref, wf_ref)
    b = branch(od_ref, dz_ref, wd_ref)
    merged = _sigmoid(gf_ref[...]) * a + _sigmoid(gd_ref[...]) * b
    out = _dot(merged.astype(BF16), wo_ref[...])
    nrm = out * lax.rsqrt(jnp.mean(out * out, axis=-1, keepdims=True) + RMS_EPS) * gp_ref[...]
    y_ref[...] = x_ref[...] + gate_ref[...] * nrm


def _mix_out(x2, gate, o_fox, fz, o_dsa, dz, gf, gd, w_fox_out, w_dsa_out, w_out, g_post, *, head_major, seq, nh, dh):
    n, d = x2.shape
    wf = nh * dh
    tm = min(256, n)
    assert n % tm == 0
    row = lambda i: (i, 0)
    if head_major:
        nt = seq // tm
        o_spec = pl.BlockSpec((None, nh, tm, dh), lambda i: (i // nt, 0, i % nt, 0))
        gate_spec = pl.BlockSpec((None, 1, d), lambda i: (i // nt, 0, 0))
    else:
        o_spec = pl.BlockSpec((tm, wf), row)
        gate_spec = pl.BlockSpec((tm, d), row)
    return pl.pallas_call(
        functools.partial(_out_kernel, head_major=head_major, nh=nh, dh=dh),
        grid=(n // tm,),
        in_specs=[pl.BlockSpec((tm, d), row), gate_spec, o_spec, pl.BlockSpec((tm, wf), row),
                  o_spec, pl.BlockSpec((tm, wf), row), pl.BlockSpec((tm, d), row), pl.BlockSpec((tm, d), row),
                  _resident((wf, d), lambda i: (0, 0)), _resident((wf, d), lambda i: (0, 0)),
                  _resident((d, d), lambda i: (0, 0)), _resident((1, d), lambda i: (0, 0))],
        out_specs=pl.BlockSpec((tm, d), row),
        out_shape=jax.ShapeDtypeStruct((n, d), F32),
        compiler_params=_cparams(("arbitrary",)),
        name="mix_out_prompt" if head_major else "mix_out_sample",
    )(x2, gate, o_fox, fz, o_dsa, dz, gf, gd, w_fox_out.astype(BF16), w_dsa_out.astype(BF16), w_out.astype(BF16),
      g_post.reshape(1, d))


def _sidx_kernel(pt_ref, q_ref, w_ref, kn_ref, *refs, n_pages, t_new, past):
    pages = refs[:n_pages]
    o_ref = refs[n_pages]
    q = q_ref[...]
    w = w_ref[...]

    def slab(kt):
        r = jnp.maximum(_dot(q, kt.astype(BF16)), 0.0) * w
        return jnp.concatenate([jnp.sum(r[8 * t:8 * t + 8], axis=0, keepdims=True) for t in range(t_new)], axis=0)

    for j in range(n_pages):
        o_ref[:, j * LANES:(j + 1) * LANES] = slab(pages[j][...])
    new = slab(kn_ref[...])
    lane = lax.broadcasted_iota(I32, (t_new, LANES), 1)
    trow = lax.broadcasted_iota(I32, (t_new, LANES), 0)
    o_ref[:, past:past + LANES] = jnp.where(lane <= trow, new, NEG_INF)
    pad = o_ref.shape[-1] - past - LANES
    if pad:
        o_ref[:, past + LANES:] = jnp.full((t_new, pad), NEG_INF, F32)


def _sample_idx_scores(page_table, q32, w32, ik_new_pad, cache_idx_k, layer, lpad):
    dbsz, n_pages = page_table.shape
    t_new = q32.shape[1] // 8
    di = q32.shape[2]
    page = cache_idx_k.shape[3]
    assert page == LANES
    page_specs = [pl.BlockSpec((None, None, di, page), (lambda b, pt, j=j: (layer, pt[b, j], 0, 0))) for j in range(n_pages)]
    return pl.pallas_call(
        functools.partial(_sidx_kernel, n_pages=n_pages, t_new=t_new, past=n_pages * page),
        grid_spec=pltpu.PrefetchScalarGridSpec(
            num_scalar_prefetch=1,
            grid=(dbsz,),
            in_specs=[pl.BlockSpec((None, t_new * 8, di), lambda b, pt: (b, 0, 0)),
                      pl.BlockSpec((None, t_new * 8, LANES), lambda b, pt: (b, 0, 0)),
                      pl.BlockSpec((None, di, page), lambda b, pt: (b, 0, 0))] + page_specs,
            out_specs=pl.BlockSpec((None, t_new, lpad), lambda b, pt: (b, 0, 0)),
        ),
        out_shape=jax.ShapeDtypeStruct((dbsz, t_new, lpad), F32),
        compiler_params=_cparams(("arbitrary",)),
        name="sample_idx_scores",
    )(page_table, q32, w32, ik_new_pad, *([cache_idx_k] * n_pages))


def _ssel_kernel(s_ref, o_ref, sc_ref, *, rows, t_new, past, topk, idx_bits):
    nslab = sc_ref.shape[0]
    lane = lax.broadcasted_iota(I32, (rows, LANES), 1)
    t = lax.broadcasted_iota(I32, (rows, LANES), 0) % t_new
    for s in range(nslab):
        valid = (s * LANES + lane) <= past + t
        sc_ref[s] = jnp.where(valid, _sort_key(s_ref[:, s * LANES:(s + 1) * LANES]), INT_MIN)
    _select_topk(sc_ref, nslab // 4, rows, topk, idx_bits)
    for s in range(nslab):
        o_ref[:, s * LANES:(s + 1) * LANES] = _mask_slab(sc_ref, s)


def _sample_select(scores2, t_new, past, topk):
    n, lpad = scores2.shape
    rows = min(n, LANES)
    assert n % rows == 0 and rows % t_new == 0 and lpad % (4 * LANES) == 0
    nslab = lpad // LANES
    return pl.pallas_call(
        functools.partial(_ssel_kernel, rows=rows, t_new=t_new, past=past, topk=topk, idx_bits=max(1, (lpad - 1).bit_length())),
        grid=(n // rows,),
        in_specs=[pl.BlockSpec((rows, lpad), lambda i: (i, 0))],
        out_specs=pl.BlockSpec((rows, lpad), lambda i: (i, 0)),
        out_shape=jax.ShapeDtypeStruct((n, lpad), F32),
        scratch_shapes=[pltpu.VMEM((nslab, rows, LANES), I32)],
        compiler_params=_cparams(("arbitrary",)),
        name="sample_select",
    )(scores2)


def _wt_kernel(lp_ref, o_ref):
    pb = lp_ref.shape[0]
    lp = lp_ref[...].reshape(pb * 8, LANES)
    r = lax.broadcasted_iota(I32, (LANES, 2 * LANES), 0)
    c = lax.broadcasted_iota(I32, (LANES, 2 * LANES), 1)
    u = jnp.where((c >= LANES) | (r > c), 1.0, 0.0).astype(BF16)
    hi, mid, lo = _split3(lp)
    res = _dot(lo, u) + _dot(mid, u) + _dot(hi, u)
    o_ref[:, 0:8, :] = res[:, :LANES].reshape(pb, 8, LANES)
    o_ref[:, 8:16, :] = res[:, LANES:].reshape(pb, 8, LANES)


def _page_suffix(logf_t):
    n_phys = logf_t.shape[0]
    pb = 256
    while n_phys % pb:
        pb //= 2
    return pl.pallas_call(
        _wt_kernel,
        grid=(n_phys // pb,),
        in_specs=[pl.BlockSpec((pb, 8, LANES), lambda i: (i, 0, 0))],
        out_specs=pl.BlockSpec((pb, 16, LANES), lambda i: (i, 0, 0)),
        out_shape=jax.ShapeDtypeStruct((n_phys, 16, LANES), F32),
        compiler_params=_cparams(("arbitrary",)),
        name="page_suffix",
    )(logf_t)


def _sattn_kernel(pt_ref, qf_ref, qd_ref, knf_ref, vnf_ref, knd_ref, vnd_ref, cn_ref, bnf_ref, selb_ref, seln_ref,
                  t5_ref, t5n_ref, *refs, ppc, nc, t_new, nh, dh):
    kf_p, vf_p, kd_p, vd_p, wt_p = (refs[i * ppc:(i + 1) * ppc] for i in range(5))
    of_ref, od_ref = refs[5 * ppc:5 * ppc + 2]
    qbf, qbd, mf, lf, af, md, ld, ad, car = refs[5 * ppc + 2:]
    c = pl.program_id(1)
    rows = t_new * 8
    wf = nh * dh
    headmask = (lax.broadcasted_iota(I32, (8, wf), 1) // dh) == lax.broadcasted_iota(I32, (8, wf), 0)

    def expand_rows(x):
        return jnp.concatenate([jnp.broadcast_to(x[t:t + 1], (8, x.shape[1])) for t in range(t_new)], axis=0)

    def step(state, q, ks, bias, vs):
        m_ref, l_ref, a_ref = state
        s = jnp.concatenate([_dot(q, k.astype(BF16)) for k in ks], axis=1) + bias
        m_old = m_ref[...]
        m_new = jnp.maximum(m_old, jnp.max(s, axis=1, keepdims=True))
        alpha = jnp.exp(m_old - m_new)
        p = jnp.exp(s - m_new)
        pb = p.astype(BF16)
        pv = None
        for j, v in enumerate(vs):
            t = _dot_nt(pb[:, j * LANES:(j + 1) * LANES], v.astype(BF16))
            pv = t if pv is None else pv + t
        m_ref[...] = m_new
        l_ref[...] = alpha * l_ref[...] + jnp.sum(p, axis=1, keepdims=True)
        a_ref[...] = alpha * a_ref[...] + pv

    @pl.when(c == 0)
    def _():
        hm = jnp.where(headmask, 1.0, 0.0)
        qbf[...] = (expand_rows(qf_ref[...].astype(F32)) * jnp.concatenate([hm] * t_new, axis=0)).astype(BF16)
        qbd[...] = (expand_rows(qd_ref[...].astype(F32)) * jnp.concatenate([hm] * t_new, axis=0)).astype(BF16)
        for m_ref, l_ref, a_ref in ((mf, lf, af), (md, ld, ad)):
            m_ref[...] = jnp.full(m_ref.shape, M_INIT, F32)
            l_ref[...] = jnp.zeros(l_ref.shape, F32)
            a_ref[...] = jnp.zeros(a_ref.shape, F32)
        car[...] = jnp.zeros(car.shape, F32)
        step((mf, lf, af), qbf[...], [knf_ref[...]], bnf_ref[...], [vnf_ref[...]])
        step((md, ld, ad), qbd[...], [knd_ref[...]], expand_rows(seln_ref[...]) + t5n_ref[...], [vnd_ref[...]])

    carry = car[...]
    cn = cn_ref[...]
    bias = [None] * ppc
    for j in reversed(range(ppc)):
        wt = wt_p[j][...]
        suf = wt[0:8] + carry
        bias[j] = jnp.concatenate([suf] * t_new, axis=0) + cn
        carry = carry + wt[8:16]
    car[...] = carry
    step((mf, lf, af), qbf[...], [r[...] for r in kf_p], jnp.concatenate(bias, axis=1), [r[...] for r in vf_p])
    step((md, ld, ad), qbd[...], [r[...] for r in kd_p], expand_rows(selb_ref[...]) + t5_ref[nc - 1 - c],
         [r[...] for r in vd_p])

    @pl.when(c == nc - 1)
    def _():
        hm = jnp.concatenate([jnp.where(headmask, 1.0, 0.0)] * t_new, axis=0)
        for (l_ref, a_ref), o_ref in (((lf, af), of_ref), ((ld, ad), od_ref)):
            o = (a_ref[...] / l_ref[...]) * hm
            o_ref[...] = jnp.concatenate([jnp.sum(o[8 * t:8 * t + 8], axis=0, keepdims=True) for t in range(t_new)], axis=0)


def _pages_per_step(n_pages):
    ppc = PAGES_PER_STEP
    while n_pages % ppc:
        ppc //= 2
    return ppc


def _sample_attention(page_table, layer, qf, qd, knf, vnf, knd, vnd, cn32, bnf, selb, seln, t5c, t5n,
                      cache_fox_k, cache_fox_v, cache_dsa_k, cache_dsa_v, wt, *, nh, dh):
    dbsz, n_pages = page_table.shape
    t_new = qf.shape[1]
    wf = nh * dh
    page = cache_fox_k.shape[3]
    ppc = _pages_per_step(n_pages)
    nc = n_pages // ppc
    rows = t_new * 8
    cwid = ppc * page
    assert page == LANES and nh == 8

    def pg(j):
        return lambda b, c, pt: (layer, pt[b, (nc - 1 - c) * ppc + j], 0, 0)

    def wtm(j):
        return lambda b, c, pt: (pt[b, (nc - 1 - c) * ppc + j], 0, 0)

    kv_specs = [pl.BlockSpec((None, None, wf, page), pg(j)) for j in range(ppc)]
    wt_specs = [pl.BlockSpec((None, 16, LANES), wtm(j)) for j in range(ppc)]
    per_b3 = lambda shp: pl.BlockSpec((None,) + shp, lambda b, c, pt: (b, 0, 0))
    caches = lambda a: [a] * ppc
    return pl.pallas_call(
        functools.partial(_sattn_kernel, ppc=ppc, nc=nc, t_new=t_new, nh=nh, dh=dh),
        grid_spec=pltpu.PrefetchScalarGridSpec(
            num_scalar_prefetch=1,
            grid=(dbsz, nc),
            in_specs=[per_b3((t_new, wf)), per_b3((t_new, wf)),
                      per_b3((wf, page)), per_b3((wf, page)), per_b3((wf, page)), per_b3((wf, page)),
                      per_b3((rows, LANES)), per_b3((rows, LANES)),
                      pl.BlockSpec((None, None, t_new, cwid), lambda b, c, pt: (b, nc - 1 - c, 0, 0)),
                      per_b3((t_new, LANES)),
                      _resident(t5c.shape, lambda b, c, pt: (0, 0, 0)),
                      _resident(t5n.shape, lambda b, c, pt: (0, 0))]
                     + kv_specs * 4 + wt_specs,
            out_specs=[per_b3((t_new, wf)), per_b3((t_new, wf))],
            scratch_shapes=[pltpu.VMEM((rows, wf), BF16), pltpu.VMEM((rows, wf), BF16),
                            pltpu.VMEM((rows, 1), F32), pltpu.VMEM((rows, 1), F32), pltpu.VMEM((rows, wf), F32),
                            pltpu.VMEM((rows, 1), F32), pltpu.VMEM((rows, 1), F32), pltpu.VMEM((rows, wf), F32),
                            pltpu.VMEM((8, LANES), F32)],
        ),
        out_shape=[jax.ShapeDtypeStruct((dbsz, t_new, wf), F32)] * 2,
        compiler_params=_cparams(("arbitrary", "arbitrary")),
        name="sample_attention",
    )(page_table, qf, qd, knf, vnf, knd, vnd, cn32, bnf, selb, seln, t5c, t5n,
      *caches(cache_fox_k), *caches(cache_fox_v), *caches(cache_dsa_k), *caches(cache_dsa_v), *([wt] * ppc))


def kernel(x_prompt, x_sample, cache_fox_k, cache_fox_v, cache_fox_logf, cache_dsa_k, cache_dsa_v, cache_idx_k,
           page_table, c_prompt, c_sample, g_pre, w_mod, b_mod, w_in, b_forget, rel_bias,
           w_fox_out, w_dsa_out, w_out, g_post):
    bsz, seq, d = x_prompt.shape
    dbsz, t_new, _ = x_sample.shape
    depth, n_phys, page, nh, dh = cache_fox_k.shape
    di = cache_idx_k.shape[-1]
    kv_t = lambda a: a.transpose(0, 1, 3, 4, 2).reshape(depth, n_phys, nh * dh, page)
    cache_fox_k, cache_fox_v, cache_dsa_k, cache_dsa_v = map(kv_t, (cache_fox_k, cache_fox_v, cache_dsa_k, cache_dsa_v))
    cache_idx_k = cache_idx_k.transpose(0, 1, 3, 2)
    wf = nh * dh
    nhi = w_in.shape[-1] - (8 * wf + nh + di + 2 * d)
    nhi = nhi // (di + 1)
    n_pages = page_table.shape[1]
    past = n_pages * page
    n_buckets = rel_bias.shape[0]
    attn_scale = dh ** -0.5
    idx_w_scale = (nhi * di) ** -0.5
    geom = dict(nh=nh, dh=dh, nhi=nhi, di=di, attn_scale=attn_scale, idx_w_scale=idx_w_scale)
    assert nh == 8 and nhi == 8 and page == LANES

    new_p = [[] for _ in range(6)]
    new_s = [[] for _ in range(6)]
    h_p, h_s = x_prompt, x_sample
    n_c = bsz + dbsz
    n_c_pad = -(-n_c // 8) * 8
    c_all = jnp.pad(jnp.concatenate([c_prompt, c_sample], axis=0), ((0, n_c_pad - n_c), (0, 0)))
    topk_s = min(TOPK_MAX, (past + t_new) // 4)
    lpad = past + 4 * LANES

    for l in range(depth):
        mod = _mod(c_all, w_mod[l], b_mod[l])
        shift, scale, gate = mod[:, :d], mod[:, d:2 * d], mod[:, 2 * d:]
        rep = lambda a: jnp.repeat(a[bsz:n_c], t_new, axis=0)

        x2 = h_p.reshape(bsz * seq, d)
        (fk_t, fv_t, lf_t, dk_t, dv_t, ik_t, fz, dz, gf, gd, iw, qa, ka, fv_h, dq_h, dk_c, dv_h, iq_h, ik_c) = _proj(
            x2, scale[:bsz, None, :], shift[:bsz, None, :], g_pre[l], w_in[l], b_forget[l],
            prompt=True, bsz=bsz, seq=seq, **geom)
        o_fox = _fox_prompt(qa, ka, fv_h, dh)
        dk_c = dk_c.reshape(bsz, seq // DSA_CHUNK, nh, dh, DSA_CHUNK)
        o_dsa = _dsa_prompt(iq_h, iw, ik_c, dq_h, dk_c, dv_h, rel_bias)
        y2 = _mix_out(x2, gate[:bsz, None, :], o_fox, fz, o_dsa, dz, gf, gd, w_fox_out[l], w_dsa_out[l], w_out[l],
                      g_post[l], head_major=True, seq=seq, nh=nh, dh=dh)
        h_p = y2.reshape(bsz, seq, d)
        for lst, a, shp in zip(new_p, (fk_t, fv_t, lf_t, dk_t, dv_t, ik_t),
                               ((nh, dh), (nh, dh), (nh,), (nh, dh), (nh, dh), (di,))):
            a = a.reshape((bsz,) + shp + (seq,))
            lst.append(jnp.moveaxis(a, -1, 1))

        xs2 = h_s.reshape(dbsz * t_new, d)
        (sfk, sfv, slf, sdk, sdv, sik, sfz, sdz, sgf, sgd, siw, sfq, sdq, siq, scn) = _proj(
            xs2, rep(scale), rep(shift), g_pre[l], w_in[l], b_forget[l],
            prompt=False, bsz=dbsz, seq=t_new, **geom)
        q32 = siq.reshape(dbsz, t_new * nhi, di)
        w32 = jnp.broadcast_to(siw.reshape(dbsz, t_new * nhi, 1), (dbsz, t_new * nhi, LANES))
        pad_new = lambda a: jnp.pad(a.reshape(dbsz, t_new, -1).transpose(0, 2, 1), ((0, 0), (0, 0), (0, page - t_new)))
        scores = _sample_idx_scores(page_table, q32, w32, pad_new(sik), cache_idx_k, l, lpad)
        sel = _sample_select(scores.reshape(dbsz * t_new, lpad), t_new, past, topk_s).reshape(dbsz, t_new, lpad)
        tpos = jnp.arange(t_new)
        rel = past + tpos[:, None] - jnp.arange(past + page)[None, :]
        t5 = rel_bias[_t5_bucket(rel, n_buckets)].astype(F32)
        t5 = t5.transpose(0, 2, 1).reshape(t_new * nh, past + page)
        ppc = _pages_per_step(n_pages)
        nc = n_pages // ppc
        t5c =t5[:, :past].reshape(t_new * nh, nc, ppc * page).transpose(1, 0, 2)
        t5n = t5[:, past:]
        selb = sel[:, :, :past].reshape(dbsz, t_new, nc, ppc * page).transpose(0, 2, 1, 3)
        seln = sel[:, :, past:past + page]
        cn = scn[:, :nh].reshape(dbsz, t_new, nh)
        cn32 = jnp.broadcast_to(cn.reshape(dbsz, t_new * nh, 1), (dbsz, t_new * nh, LANES))
        dcn = cn[:, :, None, :] - cn[:, None, :, :]
        dcn = jnp.where((tpos[None, :] <= tpos[:, None])[None, :, :, None], dcn, NEG_INF)
        bnf = jnp.pad(dcn.transpose(0, 1, 3, 2).reshape(dbsz, t_new * nh, t_new), ((0, 0), (0, 0), (0, LANES - t_new)),
                      constant_values=NEG_INF)
        wt = _page_suffix(cache_fox_logf[l].transpose(0, 2, 1))
        so_fox, so_dsa = _sample_attention(
            page_table, l, sfq.reshape(dbsz, t_new, wf), sdq.reshape(dbsz, t_new, wf),
            pad_new(sfk), pad_new(sfv), pad_new(sdk), pad_new(sdv), cn32, bnf, selb, seln, t5c, t5n,
            cache_fox_k, cache_fox_v, cache_dsa_k, cache_dsa_v, wt, nh=nh, dh=dh)
        ys2 = _mix_out(xs2, rep(gate), so_fox.reshape(dbsz * t_new, wf), sfz, so_dsa.reshape(dbsz * t_new, wf), sdz,
                       sgf, sgd, w_fox_out[l], w_dsa_out[l], w_out[l], g_post[l], head_major=False, seq=t_new,
                       nh=nh, dh=dh)
        h_s = ys2.reshape(dbsz, t_new, d)
        for lst, a, shp in zip(new_s, (sfk, sfv, slf, sdk, sdv, sik),
                               ((nh, dh), (nh, dh), (nh,), (nh, dh), (nh, dh), (di,))):
            lst.append(a.reshape((dbsz, t_new) + shp))

    return (h_p, h_s) + tuple(jnp.stack(v) for v in new_p) + tuple(jnp.stack(v) for v in new_s)
```

```python
import functools

import numpy as np
import jax
import jax.numpy as jnp
from jax import lax
from jax.experimental import pallas as pl
from jax.experimental.pallas import tpu as pltpu

F32 = jnp.float32
BF16 = jnp.bfloat16
I32 = jnp.int32

LANES = 128
TOPK_MAX = 256
MAX_DISTANCE = 128
RMS_EPS = 1e-6
NEG_INF = float("-inf")
M_INIT = -1e30
LOG2E = 1.4426950408889634
INT_MIN = -(2 ** 31)
INT_MAX = 2 ** 31 - 1
N_AUG = 3
VMEM_LIMIT = 60 * 1024 * 1024
HEADS_PER_STEP = 2
FOX_HEADS_PER_STEP = 4
SCORE_TILE = 4 * LANES
DSA_CHUNK = 8 * LANES
PAGES_PER_STEP = 16

NT = (((1,), (1,)), ((), ()))


def _cparams(sem):
    return pltpu.CompilerParams(dimension_semantics=sem, vmem_limit_bytes=VMEM_LIMIT)


def _resident(block_shape, index_map):
    return pl.BlockSpec(block_shape, index_map, pipeline_mode=pl.Buffered(1))


def _split3(x):
    hi = x.astype(BF16)
    r1 = x - hi.astype(F32)
    mid = r1.astype(BF16)
    lo = (r1 - mid.astype(F32)).astype(BF16)
    return hi, mid, lo


def _dot(a, b):
    return jnp.dot(a, b, preferred_element_type=F32)


def _dot_nt(a, b):
    return lax.dot_general(a, b, NT, preferred_element_type=F32)


def _sigmoid(x):
    return 1.0 / (1.0 + jnp.exp(-x))


def _silu(x):
    return x * _sigmoid(x)


def _mod_kernel(c_ref, w_ref, b_ref, o_ref):
    a = _silu(c_ref[...])
    a_hi, a_mid, _ = _split3(a)
    w_hi, w_mid, _ = _split3(w_ref[...])
    o_ref[...] = _dot(a_hi, w_hi) + (_dot(a_hi, w_mid) + _dot(a_mid, w_hi)) + b_ref[...]


def _mod(c, w_mod, b_mod):
    n, d = c.shape
    e = w_mod.shape[1]
    te = 512
    assert e % te == 0 and n % 8 == 0
    return pl.pallas_call(
        _mod_kernel,
        grid=(e // te,),
        in_specs=[pl.BlockSpec((n, d), lambda j: (0, 0)),
                  pl.BlockSpec((d, te), lambda j: (0, j)),
                  pl.BlockSpec((1, te), lambda j: (0, j))],
        out_specs=pl.BlockSpec((n, te), lambda j: (0, j)),
        out_shape=jax.ShapeDtypeStruct((n, e), F32),
        compiler_params=_cparams(("arbitrary",)),
        name="mod",
    )(c, w_mod, b_mod.reshape(1, e))


def _proj_kernel(x_ref, sc_ref, sh_ref, g_ref, w_ref, bf_ref, pl_ref, *refs,
                 cols, prompt, group, tm, nh, dh, nhi, di, idx_w_scale):
    if prompt:
        (fk_o, fv_o, lf_o, dk_o, dv_o, ik_o, fz_o, dz_o, gf_o, gd_o, iw_o,
         qa_o, ka_o, fvh_o, dqh_o, dkc_o, dvh_o, iqh_o, ikc_o, carry) = refs
    else:
        (fk_o, fv_o, lf_o, dk_o, dv_o, ik_o, fz_o, dz_o, gf_o, gd_o, iw_o,
         fq_o, dq_o, iq_o, cn_o, carry) = refs
    i = pl.program_id(0)
    x = x_ref[...]
    y = x * lax.rsqrt(jnp.mean(x * x, axis=-1, keepdims=True) + RMS_EPS) * g_ref[...]
    hb = (y * (1.0 + sc_ref[...]) + sh_ref[...]).astype(BF16)

    def mm(name):
        c0, c1 = cols[name]
        return _dot(hb, w_ref[:, c0:c1])

    fk = mm("fk"); fv = mm("fv"); dk = mm("dk"); dv = mm("dv")
    fz_o[...] = mm("fz")
    dz_o[...] = mm("dz")
    gf_o[...] = mm("gf")
    gd_o[...] = mm("gd")
    small = mm("small")
    iw_o[...] = small[:, 8:8 + nhi] * idx_w_scale
    z = small + bf_ref[...]
    logf_full = jnp.minimum(z, 0.0) - jnp.log(1.0 + jnp.exp(-jnp.abs(z)))
    lane = lax.broadcasted_iota(I32, (tm, LANES), 1)
    logf = jnp.where(lane < nh, logf_full, 0.0)
    if prompt:
        fk_o[...] = fk.T
        fv_o[...] = fv.T
        dkt = dk.T
        dk_o[...] = dkt
        dkc_o[...] = dkt.astype(BF16)
        dv_o[...] = dv.T
        lf_o[...] = logf.T[:nh]
        ikt = small.T[LANES - di:]
        ik_o[...] = ikt
        ikc_o[...] = ikt.astype(BF16)
    else:
        fk_o[...] = fk
        fv_o[...] = fv
        dk_o[...] = dk
        dv_o[...] = dv
        lf_o[...] = logf[:, :nh]
        ik_o[...] = small[:, LANES - di:]

    r = lax.broadcasted_iota(I32, (tm, tm), 0)
    c = lax.broadcasted_iota(I32, (tm, tm), 1)
    g_in = min(group, tm)
    tri = jnp.where((c <= r) & ((c // g_in) == (r // g_in)), 1.0, 0.0).astype(BF16)
    l_hi, l_mid, l_lo = _split3(logf)
    cum = _dot(tri, l_lo) + _dot(tri, l_mid) + _dot(tri, l_hi)
    if group > tm:
        @pl.when((i * tm) % group == 0)
        def _():
            carry[...] = jnp.zeros_like(carry)
        cum = cum + carry[...]
        carry[...] = cum[tm - 1:tm, :]

    if prompt:
        c_hi, c_mid, c_lo = _split3(cum * LOG2E)
        aug = _dot(c_hi, pl_ref[0]) + _dot(c_mid, pl_ref[1]) + _dot(c_lo, pl_ref[2]) + pl_ref[3][0:1, :].astype(F32)
        wa = nh * LANES
        qa = mm("qa") + aug[:, :wa]
        ka = mm("ka") + aug[:, wa:]
        dq = mm("dq"); iq = mm("iq")
        ones_pad = jnp.where(lax.broadcasted_iota(I32, (tm, LANES - dh), 1) == 0, 1.0, 0.0)
        for h in range(nh):
            qa_o[h] = qa[:, h * LANES:(h + 1) * LANES].astype(BF16)
            ka_o[h] = ka[:, h * LANES:(h + 1) * LANES].astype(BF16)
            fvh_o[h] = jnp.concatenate([fv[:, h * dh:(h + 1) * dh], ones_pad], axis=1).astype(BF16)
            dqh_o[h] = dq[:, h * dh:(h + 1) * dh].astype(BF16)
            dvh_o[h] = jnp.concatenate([dv[:, h * dh:(h + 1) * dh], ones_pad], axis=1).astype(BF16)
        for h in range(nhi):
            iqh_o[h] = iq[:, h * di:(h + 1) * di].astype(BF16)
    else:
        fq_o[...] = mm("fq").astype(BF16)
        dq_o[...] = mm("dq").astype(BF16)
        iq_o[...] = mm("iq").astype(BF16)
        cn_o[...] = cum


def _proj_weights(w_in, b_forget, prompt, nh, dh, nhi, di, d, attn_scale):
    wf = nh * dh
    sizes = [wf, wf, wf, nh, wf, wf, wf, wf, nhi * di, di, nhi, wf, d, d]
    offs = np.cumsum([0] + sizes)
    fq, fk, fv, ff, fz, dq, dk, dv, iq, ik, iw, dz, gf, gd = [w_in[:, offs[j]:offs[j + 1]] for j in range(14)]
    fq = fq * (attn_scale * (LOG2E if prompt else 1.0))
    dq = dq * (attn_scale * (LOG2E if prompt else 1.0))
    small = jnp.zeros((d, LANES), F32)
    small = small.at[:, 0:nh].set(ff).at[:, 8:8 + nhi].set(iw).at[:, LANES - di:].set(ik)

    def pad_heads(w):
        return jnp.pad(w.reshape(d, nh, dh), ((0, 0), (0, 0), (0, LANES - dh))).reshape(d, nh * LANES)

    if prompt:
        groups = [("qa", pad_heads(fq)), ("ka", pad_heads(fk))]
    else:
        groups = [("fq", fq)]
    groups += [("fk", fk), ("fv", fv), ("fz", fz), ("dq", dq), ("dk", dk), ("dv", dv), ("iq", iq), ("dz", dz),
               ("gf", gf), ("gd", gd), ("small", small)]
    cols, c0 = {}, 0
    for name, w in groups:
        cols[name] = (c0, c0 + w.shape[1])
        c0 += w.shape[1]
    w_all = jnp.concatenate([w for _, w in groups], axis=1).astype(BF16)
    bf_row = jnp.zeros((1, LANES), F32).at[0, 0:nh].set(b_forget)
    return w_all, cols, bf_row


def _placement(nh, dh):
    wa = nh * LANES
    p = np.zeros((N_AUG + 1, LANES, 2 * wa), np.float32)
    for h in range(nh):
        for c in range(N_AUG):
            p[c, h, h * LANES + dh + c] = 1.0
            p[c, h, wa + h * LANES + dh + N_AUG + c] = -1.0
            p[N_AUG, :, h * LANES + dh + N_AUG + c] = 1.0
            p[N_AUG, :, wa + h * LANES + dh + c] = 1.0
    return jnp.asarray(p, BF16)


def _proj(x2, scale, shift, g_pre, w_in, b_forget, *, prompt, bsz, seq, nh, dh, nhi, di, attn_scale, idx_w_scale):
    n, d = x2.shape
    wf = nh * dh
    assert dh + 2 * N_AUG <= LANES and nh <= 8 and nhi <= 8 and di <= LANES - 16
    w_all, cols, bf_row = _proj_weights(w_in, b_forget, prompt, nh, dh, nhi, di, d, attn_scale)
    place = _placement(nh, dh)
    tm = 256 if prompt else min(n, 256)
    assert n % tm == 0 and seq % min(seq, tm) == 0
    nt = seq // tm if prompt else 1
    row = lambda i: (i, 0)
    if prompt:
        mod_spec = pl.BlockSpec((None, 1, d), lambda i: (i // nt, 0, 0))
        hm = lambda i: (i // nt, 0, i % nt, 0)
    else:
        mod_spec = pl.BlockSpec((tm, d), row)
    f32o = lambda w: (jax.ShapeDtypeStruct((n, w), F32), pl.BlockSpec((tm, w), row))
    if prompt:
        f32t = lambda r: (jax.ShapeDtypeStruct((bsz, r, seq), F32), pl.BlockSpec((None, r, tm), lambda i: (i // nt, 0, i % nt)))
        outs = [f32t(wf), f32t(wf), f32t(nh), f32t(wf), f32t(wf), f32t(di)]
    else:
        outs = [f32o(wf), f32o(wf), f32o(nh), f32o(wf), f32o(wf), f32o(di)]
    outs += [f32o(wf), f32o(wf), f32o(d), f32o(d), f32o(nhi)]
    if prompt:
        hm_o = lambda heads, w: (jax.ShapeDtypeStruct((bsz, heads, seq, w), BF16), pl.BlockSpec((None, heads, tm, w), hm))

        def tiled_t(r, width):
            assert seq % width == 0 and width % tm == 0
            per = width // tm
            return (jax.ShapeDtypeStruct((bsz, seq // width, r, width), BF16),
                    pl.BlockSpec((None, None, r, tm), lambda i: (i // nt, (i % nt) // per, 0, (i % nt) % per)))

        outs += [hm_o(nh, LANES), hm_o(nh, LANES), hm_o(nh, LANES), hm_o(nh, dh), tiled_t(wf, DSA_CHUNK), hm_o(nh, LANES),
                 hm_o(nhi, di), tiled_t(di, SCORE_TILE)]
    else:
        b16o = lambda w: (jax.ShapeDtypeStruct((n, w), BF16), pl.BlockSpec((tm, w), row))
        outs += [b16o(wf), b16o(wf), b16o(nhi * di), f32o(LANES)]
    kern = functools.partial(_proj_kernel, cols=cols, prompt=prompt, group=seq, tm=tm, nh=nh, dh=dh, nhi=nhi, di=di,
                             idx_w_scale=idx_w_scale)
    return pl.pallas_call(
        kern,
        grid=(n // tm,),
        in_specs=[pl.BlockSpec((tm, d), row), mod_spec, mod_spec,
                  _resident((1, d), lambda i: (0, 0)),
                  _resident(w_all.shape, lambda i: (0, 0)),
                  _resident((1, LANES), lambda i: (0, 0)),
                  _resident(place.shape, lambda i: (0, 0, 0))],
        out_specs=[o[1] for o in outs],
        out_shape=[o[0] for o in outs],
        scratch_shapes=[pltpu.VMEM((1, LANES), F32)],
        compiler_params=_cparams(("arbitrary",)),
        name="proj_prompt" if prompt else "proj_sample",
    )(x2, scale, shift, g_pre.reshape(1, d), w_all, bf_row, place)


def _attend(carry, s, v):
    m, l, acc = carry
    m_new = jnp.maximum(m, jnp.max(s, axis=1, keepdims=True))
    alpha = jnp.exp(m - m_new)
    p = jnp.exp(s - m_new)
    l = alpha * l + jnp.sum(p, axis=1, keepdims=True)
    acc = alpha * acc + _dot(p.astype(BF16), v)
    return m_new, l, acc


def _attend_init(rows, dv):
    return (jnp.full((rows, 1), M_INIT, F32), jnp.zeros((rows, 1), F32), jnp.zeros((rows, dv), F32))


def _fold_lanes(x, op):
    out = x[:, :LANES]
    for j in range(1, x.shape[1] // LANES):
        out = op(out, x[:, j * LANES:(j + 1) * LANES])
    return out


def _fox_kernel(q_ref, k_ref, v_ref, o_ref, s_ref, mx_ref, acc_ref, *, tq, cw):
    qi = pl.program_id(2)
    nfull = (qi * tq) // cw
    hs = range(q_ref.shape[0])
    q = [q_ref[g] for g in hs]

    def logits(g, c):
        return _dot_nt(q[g], k_ref[g, pl.ds(pl.multiple_of(c * cw, cw), cw), :])

    mx_ref[...] = jnp.full(mx_ref.shape, M_INIT, F32)

    def p1(c, _):
        for g in hs:
            s = logits(g, c)
            s_ref[g, c] = s
            mx_ref[g] = jnp.maximum(mx_ref[g], _fold_lanes(s, jnp.maximum))
        return 0

    lax.fori_loop(0, nfull, p1, 0)
    row = qi * tq + lax.broadcasted_iota(I32, (tq, cw), 0)
    col = nfull * cw + lax.broadcasted_iota(I32, (tq, cw), 1)
    m = []
    for g in hs:
        s = jnp.where(col <= row, logits(g, nfull), NEG_INF)
        s_ref[g, nfull] = s
        m.append(jnp.max(jnp.maximum(mx_ref[g], _fold_lanes(s, jnp.maximum)), axis=1, keepdims=True))

    acc_ref[...] = jnp.zeros(acc_ref.shape, F32)

    def p2(c, _):
        for g in hs:
            p = jnp.exp2(s_ref[g, c] - m[g])
            acc_ref[g] += _dot(p.astype(BF16), v_ref[g, pl.ds(pl.multiple_of(c * cw, cw), cw), :])
        return 0

    lax.fori_loop(0, nfull + 1, p2, 0)
    dv = o_ref.shape[-1]
    for g in hs:
        acc = acc_ref[g]
        o_ref[g] = acc[:, :dv] / acc[:, dv:dv + 1]


def _fox_prompt(qa, ka, fv1, dv):
    bsz, nh, seq, wa = qa.shape
    wv = fv1.shape[-1]
    tq = min(256, seq)
    cw = min(1024, seq)
    g = FOX_HEADS_PER_STEP
    assert seq % cw == 0 and cw % tq == 0 and nh % g == 0 and dv < wv
    return pl.pallas_call(
        functools.partial(_fox_kernel, tq=tq, cw=cw),
        grid=(bsz, nh // g, seq // tq),
        in_specs=[pl.BlockSpec((None, g, tq, wa), lambda b, h, i: (b, h, i, 0)),
                  _resident((None, g, seq, wa), lambda b, h, i: (b, h, 0, 0)),
                  _resident((None, g, seq, wv), lambda b, h, i: (b, h, 0, 0))],
        out_specs=pl.BlockSpec((None, g, tq, dv), lambda b, h, i: (b, h, i, 0)),
        out_shape=jax.ShapeDtypeStruct((bsz, nh, seq, dv), F32),
        scratch_shapes=[pltpu.VMEM((g, seq // cw, tq, cw), F32), pltpu.VMEM((g, tq, LANES), F32),
                        pltpu.VMEM((g, tq, wv), F32)],
        compiler_params=_cparams(("arbitrary", "arbitrary", "arbitrary")),
        name="fox_prompt",
    )(qa, ka, fv1)


def _sort_key(x):
    bits = lax.bitcast_convert_type(x, I32)
    return jnp.where(bits < 0, bits ^ jnp.int32(INT_MAX), bits)


def _select_topk(sc_ref, nslab4, rows, kk, idx_bits, row0=0):
    lane = lax.broadcasted_iota(I32, (rows, LANES), 1)
    kf = jnp.float32(kk)
    rsl = slice(row0, row0 + rows)
    ngrp = -(-kk // LANES)
    assert 4 % ngrp == 0

    def count(pred):
        def body(c4, cnt):
            for j in range(4):
                s = c4 * 4 + j
                cnt = cnt + jnp.where(pred(sc_ref[s, rsl, :], s), 1.0, 0.0)
            return cnt
        cnt = lax.fori_loop(0, nslab4, body, jnp.zeros((rows, LANES), F32))
        return jnp.sum(cnt, axis=1, keepdims=True)

    def gmax(c4, g):
        g = list(g)
        for j in range(4):
            g[j % ngrp] = jnp.maximum(g[j % ngrp], sc_ref[c4 * 4 + j, rsl, :])
        return tuple(g)

    g = lax.fori_loop(0, nslab4, gmax, tuple(jnp.full((rows, LANES), INT_MIN, I32) for _ in range(ngrp)))
    g_lo, g_hi = g[0], g[0]
    for x in g[1:]:
        g_lo, g_hi = jnp.minimum(g_lo, x), jnp.maximum(g_hi, x)
    lo = jnp.min(g_lo, axis=1, keepdims=True)
    hi = jnp.max(g_hi, axis=1, keepdims=True)

    width = 32 - lax.clz(hi - lo)
    nsteps = jnp.max(width.astype(F32)).astype(I32)

    def bisect(_, lohi):
        lo, hi = lohi
        d = hi - lo
        mid = lo + lax.shift_right_logical(d, 1) + (d & 1)
        ok = count(lambda k, s: k >= mid) >= kf
        live = d != 0
        return jnp.where(live & ok, mid, lo), jnp.where(live & ~ok, mid - 1, hi)

    thr, _ = lax.fori_loop(0, nsteps, bisect, (lo, hi))
    thr = jnp.maximum(thr, INT_MIN + 1)
    c_gt = count(lambda k, s: k > thr)
    c_ge = count(lambda k, s: k >= thr)
    need = kf - c_gt

    def idx_step(i, cut):
        cand = cut + jnp.left_shift(jnp.int32(1), idx_bits - 1 - i)
        c = count(lambda k, s: (k == thr) & ((s * LANES + lane) < cand))
        return jnp.where(c < need, cand, cut)

    surplus = c_ge > kf
    cut = lax.cond(jnp.max(jnp.where(surplus, 1.0, 0.0)) > 0.0,
                   lambda: lax.fori_loop(0, idx_bits, idx_step, jnp.zeros((rows, 1), I32)),
                   lambda: jnp.zeros((rows, 1), I32))
    cut = jnp.where(surplus, cut, INT_MAX)

    def fin(c4, _):
        for j in range(4):
            s = c4 * 4 + j
            k = sc_ref[s, rsl, :]
            sel = (k > thr) | ((k == thr) & ((s * LANES + lane) <= cut))
            sc_ref[s, rsl, :] = lax.bitcast_convert_type(jnp.where(sel, 0.0, NEG_INF), I32)
        return 0

    lax.fori_loop(0, nslab4, fin, 0)


def _mask_slab(sc_ref, s):
    return lax.bitcast_convert_type(sc_ref[s], F32)


def _dsa_kernel(iq_ref, iw_ref, ik_ref, dq_ref, dk_ref, dv_ref, tab_ref, o_ref, sc_ref, s_ref, mx_ref, acc_ref,
                *, tq, topk, nh, nhi, idx_bits, spc):
    qi = pl.program_id(1)
    rb = tq // LANES
    nch = (qi * rb + rb - 1) // spc + 1
    cw = spc * LANES
    sw = SCORE_TILE
    w = iw_ref[...]
    row = qi * tq + lax.broadcasted_iota(I32, (LANES, sw), 0)

    def scores(c, _):
        kt = ik_ref[c]
        col = c * sw + lax.broadcasted_iota(I32, (LANES, sw), 1)
        for r in range(rb):
            rows = slice(r * LANES, (r + 1) * LANES)
            acc = jnp.zeros((LANES, sw), F32)
            for h in range(nhi):
                acc = acc + jnp.maximum(_dot(iq_ref[h, rows, :], kt), 0.0) * w[rows, h:h + 1]
            key = jnp.where(col <= row + r * LANES, _sort_key(acc), INT_MIN)
            for j in range(sw // LANES):
                sc_ref[c * (sw // LANES) + j, rows, :] = key[:, j * LANES:(j + 1) * LANES]
        return 0

    lax.fori_loop(0, nch * (cw // sw), scores, 0)
    for r in range(rb):
        _select_topk(sc_ref, nch * (spc // 4), LANES, topk, idx_bits, row0=r * LANES)

    hs = range(HEADS_PER_STEP)
    dv = o_ref.shape[-1]

    def heads(hp, _):
        h0 = hp * HEADS_PER_STEP
        q = [dq_ref[h0 + g] for g in hs]
        mx_ref[...] = jnp.full(mx_ref.shape, M_INIT, F32)

        def p1(c, _):
            mask = [_mask_slab(sc_ref, c * spc + j) for j in range(spc)]
            tsel = [[jnp.clip(qi * rb + r - (c * spc + j), 0, 2) for r in range(rb)] for j in range(spc)]
            for g in hs:
                bias = jnp.concatenate(
                    [mask[j] + jnp.concatenate([tab_ref[tsel[j][r], h0 + g] for r in range(rb)], axis=0)
                     for j in range(spc)], axis=1)
                kt = dk_ref[c, h0 + g]
                s = _dot(q[g], kt) + bias
                s_ref[g, c] = s
                mx_ref[g] = jnp.maximum(mx_ref[g], _fold_lanes(s, jnp.maximum))
            return 0

        lax.fori_loop(0, nch, p1, 0)
        m = [jnp.max(mx_ref[g], axis=1, keepdims=True) for g in hs]
        acc_ref[...] = jnp.zeros(acc_ref.shape, F32)

        def p2(c, _):
            ks = pl.ds(pl.multiple_of(c * cw, cw), cw)
            for g in hs:
                p = jnp.exp2(s_ref[g, c] - m[g])
                acc_ref[g] += _dot(p.astype(BF16), dv_ref[h0 + g, ks, :])
            return 0

        lax.fori_loop(0, nch, p2, 0)
        for g in hs:
            acc = acc_ref[g]
            o_ref[h0 + g] = acc[:, :dv] / acc[:, dv:dv + 1]
        return 0

    lax.fori_loop(0, nh // HEADS_PER_STEP, heads, 0)


def _t5_bucket(rel, n_buckets):
    max_exact = n_buckets // 2
    n = jnp.maximum(rel, 0)
    nf = jnp.maximum(n, 1).astype(F32)
    large = max_exact + (jnp.log(nf / max_exact) / np.log(MAX_DISTANCE / max_exact) * (n_buckets - max_exact)).astype(I32)
    return jnp.where(n < max_exact, n, jnp.minimum(large, n_buckets - 1))


def _bias_lookup(rel, rel_bias):
    n_buckets = rel_bias.shape[0]
    onehot = (_t5_bucket(rel, n_buckets)[..., None] == jnp.arange(n_buckets)).astype(F32)
    return jnp.dot(onehot, rel_bias.astype(F32), precision=lax.Precision.HIGHEST)


def _dsa_prompt(iq_h, iw, ik_c, dq_h, dk_c, dv1_h, rel_bias):
    bsz, nh, seq, dh = dq_h.shape
    nhi, di = iq_h.shape[1], iq_h.shape[3]
    wv = dv1_h.shape[-1]
    n_buckets = rel_bias.shape[0]
    tq = min(2 * LANES, seq)
    spc = DSA_CHUNK // LANES
    assert seq % DSA_CHUNK == 0 and DSA_CHUNK % SCORE_TILE == 0 and LANES == MAX_DISTANCE and dh < wv
    topk = min(TOPK_MAX, seq // 4)
    i = jnp.arange(LANES)
    rel = jnp.arange(2)[:, None, None] * LANES + i[None, :, None] - i[None, None, :]
    far = rel_bias[n_buckets - 1].astype(F32)
    tab = _bias_lookup(rel, rel_bias) - far
    tab = jnp.concatenate([tab.transpose(0, 3, 1, 2) * LOG2E, jnp.zeros((1, nh, LANES, LANES), F32)], axis=0)
    nslab = seq // LANES
    kern = functools.partial(_dsa_kernel, tq=tq, topk=topk, nh=nh, nhi=nhi, idx_bits=max(1, (seq - 1).bit_length()),
                             spc=spc)
    return pl.pallas_call(
        kern,
        grid=(bsz, seq // tq),
        in_specs=[pl.BlockSpec((None, nhi, tq, di), lambda b, i: (b, 0, i, 0)),
                  pl.BlockSpec((None, tq, nhi), lambda b, i: (b, i, 0)),
                  _resident((None,) + ik_c.shape[1:], lambda b, i: (b, 0, 0, 0)),
                  pl.BlockSpec((None, nh, tq, dh), lambda b, i: (b, 0, i, 0)),
                  _resident((None,) + dk_c.shape[1:], lambda b, i: (b, 0, 0, 0, 0)),
                  _resident((None, nh, seq, wv), lambda b, i: (b, 0, 0, 0)),
                  _resident(tab.shape, lambda b, i: (0, 0, 0, 0))],
        out_specs=pl.BlockSpec((None, nh, tq, dh), lambda b, i: (b, 0, i, 0)),
        out_shape=jax.ShapeDtypeStruct((bsz, nh, seq, dh), F32),
        scratch_shapes=[pltpu.VMEM((nslab, tq, LANES), I32),
                        pltpu.VMEM((HEADS_PER_STEP, nslab // spc, tq, spc * LANES), F32),
                        pltpu.VMEM((HEADS_PER_STEP, tq, LANES), F32), pltpu.VMEM((HEADS_PER_STEP, tq, wv), F32)],
        compiler_params=_cparams(("arbitrary", "arbitrary")),
        name="dsa_prompt",
    )(iq_h, iw.reshape(bsz, seq, nhi), ik_c, dq_h, dk_c, dv1_h, tab)


def _out_kernel(x_ref, gate_ref, of_ref, fz_ref, od_ref, dz_ref, gf_ref, gd_ref, wf_ref, wd_ref, wo_ref, gp_ref, y_ref,
                *, head_major, nh, dh):
    def branch(o_ref, z_ref, w_ref):
        g = _silu(z_ref[...])
        if head_major:
            a = None
            for h in range(nh):
                t = _dot((o_ref[h] * g[:, h * dh:(h + 1) * dh]).astype(BF16), w_ref[h * dh:(h + 1) * dh, :])
                a = t if a is None else a + t
            return a
        return _dot((o_ref[...] * g).astype(BF16), w_ref[...])

    a = branch(of_ref, fz_ref, wf_ref)
    b = branch(od_ref, dz_ref, wd_ref)
    merged = _sigmoid(gf_ref[...]) * a + _sigmoid(gd_ref[...]) * b
    out = _dot(merged.astype(BF16), wo_ref[...])
    nrm = out * lax.rsqrt(jnp.mean(out * out, axis=-1, keepdims=True) + RMS_EPS) * gp_ref[...]
    y_ref[...] = x_ref[...] + gate_ref[...] * nrm


def _mix_out(x2, gate, o_fox, fz, o_dsa, dz, gf, gd, w_fox_out, w_dsa_out, w_out, g_post, *, head_major, seq, nh, dh):
    n, d = x2.shape
    wf = nh * dh
    tm = min(256, n)
    assert n % tm == 0
    row = lambda i: (i, 0)
    if head_major:
        nt = seq // tm
        o_spec = pl.BlockSpec((None, nh, tm, dh), lambda i: (i // nt, 0, i % nt, 0))
        gate_spec = pl.BlockSpec((None, 1, d), lambda i: (i // nt, 0, 0))
    else:
        o_spec = pl.BlockSpec((tm, wf), row)
        gate_spec = pl.BlockSpec((tm, d), row)
    return pl.pallas_call(
        functools.partial(_out_kernel, head_major=head_major, nh=nh, dh=dh),
        grid=(n // tm,),
        in_specs=[pl.BlockSpec((tm, d), row), gate_spec, o_spec, pl.BlockSpec((tm, wf), row),
                  o_spec, pl.BlockSpec((tm, wf), row), pl.BlockSpec((tm, d), row), pl.BlockSpec((tm, d), row),
                  _resident((wf, d), lambda i: (0, 0)), _resident((wf, d), lambda i: (0, 0)),
                  _resident((d, d), lambda i: (0, 0)), _resident((1, d), lambda i: (0, 0))],
        out_specs=pl.BlockSpec((tm, d), row),
        out_shape=jax.ShapeDtypeStruct((n, d), F32),
        compiler_params=_cparams(("arbitrary",)),
        name="mix_out_prompt" if head_major else "mix_out_sample",
    )(x2, gate, o_fox, fz, o_dsa, dz, gf, gd, w_fox_out.astype(BF16), w_dsa_out.astype(BF16), w_out.astype(BF16),
      g_post.reshape(1, d))


def _sidx_kernel(pt_ref, q_ref, w_ref, kn_ref, *refs, n_pages, t_new, past):
    pages = refs[:n_pages]
    o_ref = refs[n_pages]
    q = q_ref[...]
    w = w_ref[...]

    def slab(kt):
        r = jnp.maximum(_dot(q, kt.astype(BF16)), 0.0) * w
        return jnp.concatenate([jnp.sum(r[8 * t:8 * t + 8], axis=0, keepdims=True) for t in range(t_new)], axis=0)

    for j in range(n_pages):
        o_ref[:, j * LANES:(j + 1) * LANES] = slab(pages[j][...])
    new = slab(kn_ref[...])
    lane = lax.broadcasted_iota(I32, (t_new, LANES), 1)
    trow = lax.broadcasted_iota(I32, (t_new, LANES), 0)
    o_ref[:, past:past + LANES] = jnp.where(lane <= trow, new, NEG_INF)
    pad = o_ref.shape[-1] - past - LANES
    if pad:
        o_ref[:, past + LANES:] = jnp.full((t_new, pad), NEG_INF, F32)


def _sample_idx_scores(page_table, q32, w32, ik_new_pad, cache_idx_k, layer, lpad):
    dbsz, n_pages = page_table.shape
    t_new = q32.shape[1] // 8
    di = q32.shape[2]
    page = cache_idx_k.shape[3]
    assert page == LANES
    page_specs = [pl.BlockSpec((None, None, di, page), (lambda b, pt, j=j: (layer, pt[b, j], 0, 0))) for j in range(n_pages)]
    return pl.pallas_call(
        functools.partial(_sidx_kernel, n_pages=n_pages, t_new=t_new, past=n_pages * page),
        grid_spec=pltpu.PrefetchScalarGridSpec(
            num_scalar_prefetch=1,
            grid=(dbsz,),
            in_specs=[pl.BlockSpec((None, t_new * 8, di), lambda b, pt: (b, 0, 0)),
                      pl.BlockSpec((None, t_new * 8, LANES), lambda b, pt: (b, 0, 0)),
                      pl.BlockSpec((None, di, page), lambda b, pt: (b, 0, 0))] + page_specs,
            out_specs=pl.BlockSpec((None, t_new, lpad), lambda b, pt: (b, 0, 0)),
        ),
        out_shape=jax.ShapeDtypeStruct((dbsz, t_new, lpad), F32),
        compiler_params=_cparams(("arbitrary",)),
        name="sample_idx_scores",
    )(page_table, q32, w32, ik_new_pad, *([cache_idx_k] * n_pages))


def _ssel_kernel(s_ref, o_ref, sc_ref, *, rows, t_new, past, topk, idx_bits):
    nslab = sc_ref.shape[0]
    lane = lax.broadcasted_iota(I32, (rows, LANES), 1)
    t = lax.broadcasted_iota(I32, (rows, LANES), 0) % t_new
    for s in range(nslab):
        valid = (s * LANES + lane) <= past + t
        sc_ref[s] = jnp.where(valid, _sort_key(s_ref[:, s * LANES:(s + 1) * LANES]), INT_MIN)
    _select_topk(sc_ref, nslab // 4, rows, topk, idx_bits)
    for s in range(nslab):
        o_ref[:, s * LANES:(s + 1) * LANES] = _mask_slab(sc_ref, s)


def _sample_select(scores2, t_new, past, topk):
    n, lpad = scores2.shape
    rows = min(n, LANES)
    assert n % rows == 0 and rows % t_new == 0 and lpad % (4 * LANES) == 0
    nslab = lpad // LANES
    return pl.pallas_call(
        functools.partial(_ssel_kernel, rows=rows, t_new=t_new, past=past, topk=topk, idx_bits=max(1, (lpad - 1).bit_length())),
        grid=(n // rows,),
        in_specs=[pl.BlockSpec((rows, lpad), lambda i: (i, 0))],
        out_specs=pl.BlockSpec((rows, lpad), lambda i: (i, 0)),
        out_shape=jax.ShapeDtypeStruct((n, lpad), F32),
        scratch_shapes=[pltpu.VMEM((nslab, rows, LANES), I32)],
        compiler_params=_cparams(("arbitrary",)),
        name="sample_select",
    )(scores2)


def _wt_kernel(lp_ref, o_ref):
    pb = lp_ref.shape[0]
    lp = lp_ref[...].reshape(pb * 8, LANES)
    r = lax.broadcasted_iota(I32, (LANES, 2 * LANES), 0)
    c = lax.broadcasted_iota(I32, (LANES, 2 * LANES), 1)
    u = jnp.where((c >= LANES) | (r > c), 1.0, 0.0).astype(BF16)
    hi, mid, lo = _split3(lp)
    res = _dot(lo, u) + _dot(mid, u) + _dot(hi, u)
    o_ref[:, 0:8, :] = res[:, :LANES].reshape(pb, 8, LANES)
    o_ref[:, 8:16, :] = res[:, LANES:].reshape(pb, 8, LANES)


def _page_suffix(logf_t):
    n_phys = logf_t.shape[0]
    pb = 256
    while n_phys % pb:
        pb //= 2
    return pl.pallas_call(
        _wt_kernel,
        grid=(n_phys // pb,),
        in_specs=[pl.BlockSpec((pb, 8, LANES), lambda i: (i, 0, 0))],
        out_specs=pl.BlockSpec((pb, 16, LANES), lambda i: (i, 0, 0)),
        out_shape=jax.ShapeDtypeStruct((n_phys, 16, LANES), F32),
        compiler_params=_cparams(("arbitrary",)),
        name="page_suffix",
    )(logf_t)


def _sattn_kernel(pt_ref, qf_ref, qd_ref, knf_ref, vnf_ref, knd_ref, vnd_ref, cn_ref, bnf_ref, selb_ref, seln_ref,
                  t5_ref, t5n_ref, *refs, ppc, nc, t_new, nh, dh):
    kf_p, vf_p, kd_p, vd_p, wt_p = (refs[i * ppc:(i + 1) * ppc] for i in range(5))
    of_ref, od_ref = refs[5 * ppc:5 * ppc + 2]
    qbf, qbd, mf, lf, af, md, ld, ad, car = refs[5 * ppc + 2:]
    c = pl.program_id(1)
    rows = t_new * 8
    wf = nh * dh
    headmask = (lax.broadcasted_iota(I32, (8, wf), 1) // dh) == lax.broadcasted_iota(I32, (8, wf), 0)

    def expand_rows(x):
        return jnp.concatenate([jnp.broadcast_to(x[t:t + 1], (8, x.shape[1])) for t in range(t_new)], axis=0)

    def step(state, q, ks, bias, vs):
        m_ref, l_ref, a_ref = state
        s = jnp.concatenate([_dot(q, k.astype(BF16)) for k in ks], axis=1) + bias
        m_old = m_ref[...]
        m_new = jnp.maximum(m_old, jnp.max(s, axis=1, keepdims=True))
        alpha = jnp.exp(m_old - m_new)
        p = jnp.exp(s - m_new)
        pb = p.astype(BF16)
        pv = None
        for j, v in enumerate(vs):
            t = _dot_nt(pb[:, j * LANES:(j + 1) * LANES], v.astype(BF16))
            pv = t if pv is None else pv + t
        m_ref[...] = m_new
        l_ref[...] = alpha * l_ref[...] + jnp.sum(p, axis=1, keepdims=True)
        a_ref[...] = alpha * a_ref[...] + pv

    @pl.when(c == 0)
    def _():
        hm = jnp.where(headmask, 1.0, 0.0)
        qbf[...] = (expand_rows(qf_ref[...].astype(F32)) * jnp.concatenate([hm] * t_new, axis=0)).astype(BF16)
        qbd[...] = (expand_rows(qd_ref[...].astype(F32)) * jnp.concatenate([hm] * t_new, axis=0)).astype(BF16)
        for m_ref, l_ref, a_ref in ((mf, lf, af), (md, ld, ad)):
            m_ref[...] = jnp.full(m_ref.shape, M_INIT, F32)
            l_ref[...] = jnp.zeros(l_ref.shape, F32)
            a_ref[...] = jnp.zeros(a_ref.shape, F32)
        car[...] = jnp.zeros(car.shape, F32)
        step((mf, lf, af), qbf[...], [knf_ref[...]], bnf_ref[...], [vnf_ref[...]])
        step((md, ld, ad), qbd[...], [knd_ref[...]], expand_rows(seln_ref[...]) + t5n_ref[...], [vnd_ref[...]])

    carry = car[...]
    cn = cn_ref[...]
    bias = [None] * ppc
    for j in reversed(range(ppc)):
        wt = wt_p[j][...]
        suf = wt[0:8] + carry
        bias[j] = jnp.concatenate([suf] * t_new, axis=0) + cn
        carry = carry + wt[8:16]
    car[...] = carry
    step((mf, lf, af), qbf[...], [r[...] for r in kf_p], jnp.concatenate(bias, axis=1), [r[...] for r in vf_p])
    step((md, ld, ad), qbd[...], [r[...] for r in kd_p], expand_rows(selb_ref[...]) + t5_ref[nc - 1 - c],
         [r[...] for r in vd_p])

    @pl.when(c == nc - 1)
    def _():
        hm = jnp.concatenate([jnp.where(headmask, 1.0, 0.0)] * t_new, axis=0)
        for (l_ref, a_ref), o_ref in (((lf, af), of_ref), ((ld, ad), od_ref)):
            o = (a_ref[...] / l_ref[...]) * hm
            o_ref[...] = jnp.concatenate([jnp.sum(o[8 * t:8 * t + 8], axis=0, keepdims=True) for t in range(t_new)], axis=0)


def _pages_per_step(n_pages):
    ppc = PAGES_PER_STEP
    while n_pages % ppc:
        ppc //= 2
    return ppc


def _sample_attention(page_table, layer, qf, qd, knf, vnf, knd, vnd, cn32, bnf, selb, seln, t5c, t5n,
                      cache_fox_k, cache_fox_v, cache_dsa_k, cache_dsa_v, wt, *, nh, dh):
    dbsz, n_pages = page_table.shape
    t_new = qf.shape[1]
    wf = nh * dh
    page = cache_fox_k.shape[3]
    ppc = _pages_per_step(n_pages)
    nc = n_pages // ppc
    rows = t_new * 8
    cwid = ppc * page
    assert page == LANES and nh == 8

    def pg(j):
        return lambda b, c, pt: (layer, pt[b, (nc - 1 - c) * ppc + j], 0, 0)

    def wtm(j):
        return lambda b, c, pt: (pt[b, (nc - 1 - c) * ppc + j], 0, 0)

    kv_specs = [pl.BlockSpec((None, None, wf, page), pg(j)) for j in range(ppc)]
    wt_specs = [pl.BlockSpec((None, 16, LANES), wtm(j)) for j in range(ppc)]
    per_b3 = lambda shp: pl.BlockSpec((None,) + shp, lambda b, c, pt: (b, 0, 0))
    caches = lambda a: [a] * ppc
    return pl.pallas_call(
        functools.partial(_sattn_kernel, ppc=ppc, nc=nc, t_new=t_new, nh=nh, dh=dh),
        grid_spec=pltpu.PrefetchScalarGridSpec(
            num_scalar_prefetch=1,
            grid=(dbsz, nc),
            in_specs=[per_b3((t_new, wf)), per_b3((t_new, wf)),
                      per_b3((wf, page)), per_b3((wf, page)), per_b3((wf, page)), per_b3((wf, page)),
                      per_b3((rows, LANES)), per_b3((rows, LANES)),
                      pl.BlockSpec((None, None, t_new, cwid), lambda b, c, pt: (b, nc - 1 - c, 0, 0)),
                      per_b3((t_new, LANES)),
                      _resident(t5c.shape, lambda b, c, pt: (0, 0, 0)),
                      _resident(t5n.shape, lambda b, c, pt: (0, 0))]
                     + kv_specs * 4 + wt_specs,
            out_specs=[per_b3((t_new, wf)), per_b3((t_new, wf))],
            scratch_shapes=[pltpu.VMEM((rows, wf), BF16), pltpu.VMEM((rows, wf), BF16),
                            pltpu.VMEM((rows, 1), F32), pltpu.VMEM((rows, 1), F32), pltpu.VMEM((rows, wf), F32),
                            pltpu.VMEM((rows, 1), F32), pltpu.VMEM((rows, 1), F32), pltpu.VMEM((rows, wf), F32),
                            pltpu.VMEM((8, LANES), F32)],
        ),
        out_shape=[jax.ShapeDtypeStruct((dbsz, t_new, wf), F32)] * 2,
        compiler_params=_cparams(("arbitrary", "arbitrary")),
        name="sample_attention",
    )(page_table, qf, qd, knf, vnf, knd, vnd, cn32, bnf, selb, seln, t5c, t5n,
      *caches(cache_fox_k), *caches(cache_fox_v), *caches(cache_dsa_k), *caches(cache_dsa_v), *([wt] * ppc))


def kernel(x_prompt, x_sample, cache_fox_k, cache_fox_v, cache_fox_logf, cache_dsa_k, cache_dsa_v, cache_idx_k,
           page_table, c_prompt, c_sample, g_pre, w_mod, b_mod, w_in, b_forget, rel_bias,
           w_fox_out, w_dsa_out, w_out, g_post):
    bsz, seq, d = x_prompt.shape
    dbsz, t_new, _ = x_sample.shape
    depth, n_phys, page, nh, dh = cache_fox_k.shape
    di = cache_idx_k.shape[-1]
    kv_t = lambda a: a.transpose(0, 1, 3, 4, 2).reshape(depth, n_phys, nh * dh, page)
    cache_fox_k, cache_fox_v, cache_dsa_k, cache_dsa_v = map(kv_t, (cache_fox_k, cache_fox_v, cache_dsa_k, cache_dsa_v))
    cache_idx_k = cache_idx_k.transpose(0, 1, 3, 2)
    wf = nh * dh
    nhi = w_in.shape[-1] - (8 * wf + nh + di + 2 * d)
    nhi = nhi // (di + 1)
    n_pages = page_table.shape[1]
    past = n_pages * page
    n_buckets = rel_bias.shape[0]
    attn_scale = dh ** -0.5
    idx_w_scale = (nhi * di) ** -0.5
    geom = dict(nh=nh, dh=dh, nhi=nhi, di=di, attn_scale=attn_scale, idx_w_scale=idx_w_scale)
    assert nh == 8 and nhi == 8 and page == LANES

    new_p = [[] for _ in range(6)]
    new_s = [[] for _ in range(6)]
    h_p, h_s = x_prompt, x_sample
    n_c = bsz + dbsz
    n_c_pad = -(-n_c // 8) * 8
    c_all = jnp.pad(jnp.concatenate([c_prompt, c_sample], axis=0), ((0, n_c_pad - n_c), (0, 0)))
    topk_s = min(TOPK_MAX, (past + t_new) // 4)
    lpad = past + 4 * LANES

    for l in range(depth):
        mod = _mod(c_all, w_mod[l], b_mod[l])
        shift, scale, gate = mod[:, :d], mod[:, d:2 * d], mod[:, 2 * d:]
        rep = lambda a: jnp.repeat(a[bsz:n_c], t_new, axis=0)

        x2 = h_p.reshape(bsz * seq, d)
        (fk_t, fv_t, lf_t, dk_t, dv_t, ik_t, fz, dz, gf, gd, iw, qa, ka, fv_h, dq_h, dk_c, dv_h, iq_h, ik_c) = _proj(
            x2, scale[:bsz, None, :], shift[:bsz, None, :], g_pre[l], w_in[l], b_forget[l],
            prompt=True, bsz=bsz, seq=seq, **geom)
        o_fox = _fox_prompt(qa, ka, fv_h, dh)
        dk_c = dk_c.reshape(bsz, seq // DSA_CHUNK, nh, dh, DSA_CHUNK)
        o_dsa = _dsa_prompt(iq_h, iw, ik_c, dq_h, dk_c, dv_h, rel_bias)
        y2 = _mix_out(x2, gate[:bsz, None, :], o_fox, fz, o_dsa, dz, gf, gd, w_fox_out[l], w_dsa_out[l], w_out[l],
                      g_post[l], head_major=True, seq=seq, nh=nh, dh=dh)
        h_p = y2.reshape(bsz, seq, d)
        for lst, a, shp in zip(new_p, (fk_t, fv_t, lf_t, dk_t, dv_t, ik_t),
                               ((nh, dh), (nh, dh), (nh,), (nh, dh), (nh, dh), (di,))):
            a = a.reshape((bsz,) + shp + (seq,))
            lst.append(jnp.moveaxis(a, -1, 1))

        xs2 = h_s.reshape(dbsz * t_new, d)
        (sfk, sfv, slf, sdk, sdv, sik, sfz, sdz, sgf, sgd, siw, sfq, sdq, siq, scn) = _proj(
            xs2, rep(scale), rep(shift), g_pre[l], w_in[l], b_forget[l],
            prompt=False, bsz=dbsz, seq=t_new, **geom)
        q32 = siq.reshape(dbsz, t_new * nhi, di)
        w32 = jnp.broadcast_to(siw.reshape(dbsz, t_new * nhi, 1), (dbsz, t_new * nhi, LANES))
        pad_new = lambda a: jnp.pad(a.reshape(dbsz, t_new, -1).transpose(0, 2, 1), ((0, 0), (0, 0), (0, page - t_new)))
        scores = _sample_idx_scores(page_table, q32, w32, pad_new(sik), cache_idx_k, l, lpad)
        sel = _sample_select(scores.reshape(dbsz * t_new, lpad), t_new, past, topk_s).reshape(dbsz, t_new, lpad)
        tpos = jnp.arange(t_new)
        rel = past + tpos[:, None] - jnp.arange(past + page)[None, :]
        t5 = _bias_lookup(rel, rel_bias)
        t5 = t5.transpose(0, 2, 1).reshape(t_new * nh, past + page)
        ppc = _pages_per_step(n_pages)
        nc = n_pages // ppc
        t5c =t5[:, :past].reshape(t_new * nh, nc, ppc * page).transpose(1, 0, 2)
        t5n = t5[:, past:]
        selb = sel[:, :, :past].reshape(dbsz, t_new, nc, ppc * page).transpose(0, 2, 1, 3)
        seln = sel[:, :, past:past + page]
        cn = scn[:, :nh].reshape(dbsz, t_new, nh)
        cn32 = jnp.broadcast_to(cn.reshape(dbsz, t_new * nh, 1), (dbsz, t_new * nh, LANES))
        dcn = cn[:, :, None, :] - cn[:, None, :, :]
        dcn = jnp.where((tpos[None, :] <= tpos[:, None])[None, :, :, None], dcn, NEG_INF)
        bnf = jnp.pad(dcn.transpose(0, 1, 3, 2).reshape(dbsz, t_new * nh, t_new), ((0, 0), (0, 0), (0, LANES - t_new)),
                      constant_values=NEG_INF)
        wt = _page_suffix(cache_fox_logf[l].transpose(0, 2, 1))
        so_fox, so_dsa = _sample_attention(
            page_table, l, sfq.reshape(dbsz, t_new, wf), sdq.reshape(dbsz, t_new, wf),
            pad_new(sfk), pad_new(sfv), pad_new(sdk), pad_new(sdv), cn32, bnf, selb, seln, t5c, t5n,
            cache_fox_k, cache_fox_v, cache_dsa_k, cache_dsa_v, wt, nh=nh, dh=dh)
        ys2 = _mix_out(xs2, rep(gate), so_fox.reshape(dbsz * t_new, wf), sfz, so_dsa.reshape(dbsz * t_new, wf), sdz,
                       sgf, sgd, w_fox_out[l], w_dsa_out[l], w_out[l], g_post[l], head_major=False, seq=t_new,
                       nh=nh, dh=dh)
        h_s = ys2.reshape(dbsz, t_new, d)
        for lst, a, shp in zip(new_s, (sfk, sfv, slf, sdk, sdv, sik),
                               ((nh, dh), (nh, dh), (nh,), (nh, dh), (nh, dh), (di,))):
            lst.append(a.reshape((dbsz, t_new) + shp))

    return (h_p, h_s) + tuple(jnp.stack(v) for v in new_p) + tuple(jnp.stack(v) for v in new_s)
```

```python
import functools

import numpy as np
import jax
import jax.numpy as jnp
from jax import lax
from jax.experimental import pallas as pl
from jax.experimental.pallas import tpu as pltpu

F32 = jnp.float32
BF16 = jnp.bfloat16
I32 = jnp.int32

LANES = 128
TOPK_MAX = 256
MAX_DISTANCE = 128
RMS_EPS = 1e-6
NEG_INF = float("-inf")
M_INIT = -1e30
LOG2E = 1.4426950408889634
INT_MIN = -(2 ** 31)
INT_MAX = 2 ** 31 - 1
N_AUG = 3
VMEM_LIMIT = 60 * 1024 * 1024
HEADS_PER_STEP = 2
FOX_HEADS_PER_STEP = 4
SCORE_TILE = 4 * LANES
DSA_CHUNK = 8 * LANES
PAGES_PER_STEP = 16

NT = (((1,), (1,)), ((), ()))


def _cparams(sem):
    return pltpu.CompilerParams(dimension_semantics=sem, vmem_limit_bytes=VMEM_LIMIT)


def _resident(block_shape, index_map):
    return pl.BlockSpec(block_shape, index_map, pipeline_mode=pl.Buffered(1))


def _split3(x):
    hi = x.astype(BF16)
    r1 = x - hi.astype(F32)
    mid = r1.astype(BF16)
    lo = (r1 - mid.astype(F32)).astype(BF16)
    return hi, mid, lo


def _dot(a, b):
    return jnp.dot(a, b, preferred_element_type=F32)


def _dot_nt(a, b):
    return lax.dot_general(a, b, NT, preferred_element_type=F32)


def _sigmoid(x):
    return 1.0 / (1.0 + jnp.exp(-x))


def _silu(x):
    return x * _sigmoid(x)


def _mod_kernel(c_ref, w_ref, b_ref, o_ref):
    a = _silu(c_ref[...])
    a_hi, a_mid, _ = _split3(a)
    w_hi, w_mid, _ = _split3(w_ref[...])
    o_ref[...] = _dot(a_hi, w_hi) + (_dot(a_hi, w_mid) + _dot(a_mid, w_hi)) + b_ref[...]


def _mod(c, w_mod, b_mod):
    n, d = c.shape
    e = w_mod.shape[1]
    te = 512
    assert e % te == 0 and n % 8 == 0
    return pl.pallas_call(
        _mod_kernel,
        grid=(e // te,),
        in_specs=[pl.BlockSpec((n, d), lambda j: (0, 0)),
                  pl.BlockSpec((d, te), lambda j: (0, j)),
                  pl.BlockSpec((1, te), lambda j: (0, j))],
        out_specs=pl.BlockSpec((n, te), lambda j: (0, j)),
        out_shape=jax.ShapeDtypeStruct((n, e), F32),
        compiler_params=_cparams(("arbitrary",)),
        name="mod",
    )(c, w_mod, b_mod.reshape(1, e))


def _proj_kernel(x_ref, sc_ref, sh_ref, g_ref, w_ref, bf_ref, pl_ref, *refs,
                 cols, prompt, group, tm, nh, dh, nhi, di, idx_w_scale):
    if prompt:
        (fk_o, fv_o, lf_o, dk_o, dv_o, ik_o, fz_o, dz_o, gf_o, gd_o, iw_o,
         qa_o, ka_o, fvh_o, dqh_o, dkc_o, dvh_o, iqh_o, ikc_o, carry) = refs
    else:
        (fk_o, fv_o, lf_o, dk_o, dv_o, ik_o, fz_o, dz_o, gf_o, gd_o, iw_o,
         fq_o, dq_o, iq_o, cn_o, carry) = refs
    i = pl.program_id(0)
    x = x_ref[...]
    y = x * lax.rsqrt(jnp.mean(x * x, axis=-1, keepdims=True) + RMS_EPS) * g_ref[...]
    hb = (y * (1.0 + sc_ref[...]) + sh_ref[...]).astype(BF16)

    def mm(name):
        c0, c1 = cols[name]
        return _dot(hb, w_ref[:, c0:c1])

    fk = mm("fk"); fv = mm("fv"); dk = mm("dk"); dv = mm("dv")
    fz_o[...] = mm("fz")
    dz_o[...] = mm("dz")
    gf_o[...] = mm("gf")
    gd_o[...] = mm("gd")
    small = mm("small")
    iw_o[...] = small[:, 8:8 + nhi] * idx_w_scale
    z = small + bf_ref[...]
    logf_full = jnp.minimum(z, 0.0) - jnp.log(1.0 + jnp.exp(-jnp.abs(z)))
    lane = lax.broadcasted_iota(I32, (tm, LANES), 1)
    logf = jnp.where(lane < nh, logf_full, 0.0)
    if prompt:
        fk_o[...] = fk.T
        fv_o[...] = fv.T
        dkt = dk.T
        dk_o[...] = dkt
        dkc_o[...] = dkt.astype(BF16)
        dv_o[...] = dv.T
        lf_o[...] = logf.T[:nh]
        ikt = small.T[LANES - di:]
        ik_o[...] = ikt
        ikc_o[...] = ikt.astype(BF16)
    else:
        fk_o[...] = fk
        fv_o[...] = fv
        dk_o[...] = dk
        dv_o[...] = dv
        lf_o[...] = logf[:, :nh]
        ik_o[...] = small[:, LANES - di:]

    r = lax.broadcasted_iota(I32, (tm, tm), 0)
    c = lax.broadcasted_iota(I32, (tm, tm), 1)
    g_in = min(group, tm)
    tri = jnp.where((c <= r) & ((c // g_in) == (r // g_in)), 1.0, 0.0).astype(BF16)
    l_hi, l_mid, l_lo = _split3(logf)
    cum = _dot(tri, l_lo) + _dot(tri, l_mid) + _dot(tri, l_hi)
    if group > tm:
        @pl.when((i * tm) % group == 0)
        def _():
            carry[...] = jnp.zeros_like(carry)
        cum = cum + carry[...]
        carry[...] = cum[tm - 1:tm, :]

    if prompt:
        c_hi, c_mid, c_lo = _split3(cum * LOG2E)
        aug = _dot(c_hi, pl_ref[0]) + _dot(c_mid, pl_ref[1]) + _dot(c_lo, pl_ref[2]) + pl_ref[3][0:1, :].astype(F32)
        wa = nh * LANES
        qa = mm("qa") + aug[:, :wa]
        ka = mm("ka") + aug[:, wa:]
        dq = mm("dq"); iq = mm("iq")
        ones_pad = jnp.where(lax.broadcasted_iota(I32, (tm, LANES - dh), 1) == 0, 1.0, 0.0)
        for h in range(nh):
            qa_o[h] = qa[:, h * LANES:(h + 1) * LANES].astype(BF16)
            ka_o[h] = ka[:, h * LANES:(h + 1) * LANES].astype(BF16)
            fvh_o[h] = jnp.concatenate([fv[:, h * dh:(h + 1) * dh], ones_pad], axis=1).astype(BF16)
            dqh_o[h] = dq[:, h * dh:(h + 1) * dh].astype(BF16)
            dvh_o[h] = jnp.concatenate([dv[:, h * dh:(h + 1) * dh], ones_pad], axis=1).astype(BF16)
        for h in range(nhi):
            iqh_o[h] = iq[:, h * di:(h + 1) * di].astype(BF16)
    else:
        fq_o[...] = mm("fq").astype(BF16)
        dq_o[...] = mm("dq").astype(BF16)
        iq_o[...] = mm("iq").astype(BF16)
        cn_o[...] = cum


def _proj_weights(w_in, b_forget, prompt, nh, dh, nhi, di, d, attn_scale):
    wf = nh * dh
    sizes = [wf, wf, wf, nh, wf, wf, wf, wf, nhi * di, di, nhi, wf, d, d]
    offs = np.cumsum([0] + sizes)
    fq, fk, fv, ff, fz, dq, dk, dv, iq, ik, iw, dz, gf, gd = [w_in[:, offs[j]:offs[j + 1]] for j in range(14)]
    fq = fq * (attn_scale * (LOG2E if prompt else 1.0))
    dq = dq * (attn_scale * (LOG2E if prompt else 1.0))
    small = jnp.zeros((d, LANES), F32)
    small = small.at[:, 0:nh].set(ff).at[:, 8:8 + nhi].set(iw).at[:, LANES - di:].set(ik)

    def pad_heads(w):
        return jnp.pad(w.reshape(d, nh, dh), ((0, 0), (0, 0), (0, LANES - dh))).reshape(d, nh * LANES)

    if prompt:
        groups = [("qa", pad_heads(fq)), ("ka", pad_heads(fk))]
    else:
        groups = [("fq", fq)]
    groups += [("fk", fk), ("fv", fv), ("fz", fz), ("dq", dq), ("dk", dk), ("dv", dv), ("iq", iq), ("dz", dz),
               ("gf", gf), ("gd", gd), ("small", small)]
    cols, c0 = {}, 0
    for name, w in groups:
        cols[name] = (c0, c0 + w.shape[1])
        c0 += w.shape[1]
    w_all = jnp.concatenate([w for _, w in groups], axis=1).astype(BF16)
    bf_row = jnp.zeros((1, LANES), F32).at[0, 0:nh].set(b_forget)
    return w_all, cols, bf_row


def _placement(nh, dh):
    wa = nh * LANES
    p = np.zeros((N_AUG + 1, LANES, 2 * wa), np.float32)
    for h in range(nh):
        for c in range(N_AUG):
            p[c, h, h * LANES + dh + c] = 1.0
            p[c, h, wa + h * LANES + dh + N_AUG + c] = -1.0
            p[N_AUG, :, h * LANES + dh + N_AUG + c] = 1.0
            p[N_AUG, :, wa + h * LANES + dh + c] = 1.0
    return jnp.asarray(p, BF16)


def _proj(x2, scale, shift, g_pre, w_in, b_forget, *, prompt, bsz, seq, nh, dh, nhi, di, attn_scale, idx_w_scale):
    n, d = x2.shape
    wf = nh * dh
    assert dh + 2 * N_AUG <= LANES and nh <= 8 and nhi <= 8 and di <= LANES - 16
    w_all, cols, bf_row = _proj_weights(w_in, b_forget, prompt, nh, dh, nhi, di, d, attn_scale)
    place = _placement(nh, dh)
    tm = 256 if prompt else min(n, 256)
    assert n % tm == 0 and seq % min(seq, tm) == 0
    nt = seq // tm if prompt else 1
    row = lambda i: (i, 0)
    if prompt:
        mod_spec = pl.BlockSpec((None, 1, d), lambda i: (i // nt, 0, 0))
        hm = lambda i: (i // nt, 0, i % nt, 0)
    else:
        mod_spec = pl.BlockSpec((tm, d), row)
    f32o = lambda w: (jax.ShapeDtypeStruct((n, w), F32), pl.BlockSpec((tm, w), row))
    if prompt:
        f32t = lambda r: (jax.ShapeDtypeStruct((bsz, r, seq), F32), pl.BlockSpec((None, r, tm), lambda i: (i // nt, 0, i % nt)))
        outs = [f32t(wf), f32t(wf), f32t(nh), f32t(wf), f32t(wf), f32t(di)]
    else:
        outs = [f32o(wf), f32o(wf), f32o(nh), f32o(wf), f32o(wf), f32o(di)]
    outs += [f32o(wf), f32o(wf), f32o(d), f32o(d), f32o(nhi)]
    if prompt:
        hm_o = lambda heads, w: (jax.ShapeDtypeStruct((bsz, heads, seq, w), BF16), pl.BlockSpec((None, heads, tm, w), hm))

        def tiled_t(r, width):
            assert seq % width == 0 and width % tm == 0
            per = width // tm
            return (jax.ShapeDtypeStruct((bsz, seq // width, r, width), BF16),
                    pl.BlockSpec((None, None, r, tm), lambda i: (i // nt, (i % nt) // per, 0, (i % nt) % per)))

        outs += [hm_o(nh, LANES), hm_o(nh, LANES), hm_o(nh, LANES), hm_o(nh, dh), tiled_t(wf, DSA_CHUNK), hm_o(nh, LANES),
                 hm_o(nhi, di), tiled_t(di, SCORE_TILE)]
    else:
        b16o = lambda w: (jax.ShapeDtypeStruct((n, w), BF16), pl.BlockSpec((tm, w), row))
        outs += [b16o(wf), b16o(wf), b16o(nhi * di), f32o(LANES)]
    kern = functools.partial(_proj_kernel, cols=cols, prompt=prompt, group=seq, tm=tm, nh=nh, dh=dh, nhi=nhi, di=di,
                             idx_w_scale=idx_w_scale)
    return pl.pallas_call(
        kern,
        grid=(n // tm,),
        in_specs=[pl.BlockSpec((tm, d), row), mod_spec, mod_spec,
                  _resident((1, d), lambda i: (0, 0)),
                  _resident(w_all.shape, lambda i: (0, 0)),
                  _resident((1, LANES), lambda i: (0, 0)),
                  _resident(place.shape, lambda i: (0, 0, 0))],
        out_specs=[o[1] for o in outs],
        out_shape=[o[0] for o in outs],
        scratch_shapes=[pltpu.VMEM((1, LANES), F32)],
        compiler_params=_cparams(("arbitrary",)),
        name="proj_prompt" if prompt else "proj_sample",
    )(x2, scale, shift, g_pre.reshape(1, d), w_all, bf_row, place)


def _attend(carry, s, v):
    m, l, acc = carry
    m_new = jnp.maximum(m, jnp.max(s, axis=1, keepdims=True))
    alpha = jnp.exp(m - m_new)
    p = jnp.exp(s - m_new)
    l = alpha * l + jnp.sum(p, axis=1, keepdims=True)
    acc = alpha * acc + _dot(p.astype(BF16), v)
    return m_new, l, acc


def _attend_init(rows, dv):
    return (jnp.full((rows, 1), M_INIT, F32), jnp.zeros((rows, 1), F32), jnp.zeros((rows, dv), F32))


def _fold_lanes(x, op):
    out = x[:, :LANES]
    for j in range(1, x.shape[1] // LANES):
        out = op(out, x[:, j * LANES:(j + 1) * LANES])
    return out


def _fox_kernel(q_ref, k_ref, v_ref, o_ref, s_ref, mx_ref, acc_ref, *, tq, cw):
    qi = pl.program_id(2)
    nfull = (qi * tq) // cw
    hs = range(q_ref.shape[0])
    q = [q_ref[g] for g in hs]

    def logits(g, c):
        return _dot_nt(q[g], k_ref[g, pl.ds(pl.multiple_of(c * cw, cw), cw), :])

    mx_ref[...] = jnp.full(mx_ref.shape, M_INIT, F32)

    def p1(c, _):
        for g in hs:
            s = logits(g, c)
            s_ref[g, c] = s
            mx_ref[g] = jnp.maximum(mx_ref[g], _fold_lanes(s, jnp.maximum))
        return 0

    lax.fori_loop(0, nfull, p1, 0)
    row = qi * tq + lax.broadcasted_iota(I32, (tq, cw), 0)
    col = nfull * cw + lax.broadcasted_iota(I32, (tq, cw), 1)
    m = []
    for g in hs:
        s = jnp.where(col <= row, logits(g, nfull), NEG_INF)
        s_ref[g, nfull] = s
        m.append(jnp.max(jnp.maximum(mx_ref[g], _fold_lanes(s, jnp.maximum)), axis=1, keepdims=True))

    acc_ref[...] = jnp.zeros(acc_ref.shape, F32)

    def p2(c, _):
        for g in hs:
            p = jnp.exp2(s_ref[g, c] - m[g])
            acc_ref[g] += _dot(p.astype(BF16), v_ref[g, pl.ds(pl.multiple_of(c * cw, cw), cw), :])
        return 0

    lax.fori_loop(0, nfull + 1, p2, 0)
    dv = o_ref.shape[-1]
    for g in hs:
        acc = acc_ref[g]
        o_ref[g] = acc[:, :dv] / acc[:, dv:dv + 1]


def _fox_prompt(qa, ka, fv1, dv):
    bsz, nh, seq, wa = qa.shape
    wv = fv1.shape[-1]
    tq = min(256, seq)
    cw = min(1024, seq)
    g = FOX_HEADS_PER_STEP
    assert seq % cw == 0 and cw % tq == 0 and nh % g == 0 and dv < wv
    return pl.pallas_call(
        functools.partial(_fox_kernel, tq=tq, cw=cw),
        grid=(bsz, nh // g, seq // tq),
        in_specs=[pl.BlockSpec((None, g, tq, wa), lambda b, h, i: (b, h, i, 0)),
                  _resident((None, g, seq, wa), lambda b, h, i: (b, h, 0, 0)),
                  _resident((None, g, seq, wv), lambda b, h, i: (b, h, 0, 0))],
        out_specs=pl.BlockSpec((None, g, tq, dv), lambda b, h, i: (b, h, i, 0)),
        out_shape=jax.ShapeDtypeStruct((bsz, nh, seq, dv), F32),
        scratch_shapes=[pltpu.VMEM((g, seq // cw, tq, cw), F32), pltpu.VMEM((g, tq, LANES), F32),
                        pltpu.VMEM((g, tq, wv), F32)],
        compiler_params=_cparams(("arbitrary", "arbitrary", "arbitrary")),
        name="fox_prompt",
    )(qa, ka, fv1)


def _sort_key(x):
    bits = lax.bitcast_convert_type(x, I32)
    return jnp.where(bits < 0, bits ^ jnp.int32(INT_MAX), bits)


def _select_topk(sc_ref, nslab4, rows, kk, idx_bits, row0=0):
    lane = lax.broadcasted_iota(I32, (rows, LANES), 1)
    kf = jnp.float32(kk)
    rsl = slice(row0, row0 + rows)
    ngrp = -(-kk // LANES)
    assert 4 % ngrp == 0

    def count(pred):
        def body(c4, cnt):
            for j in range(4):
                s = c4 * 4 + j
                cnt = cnt + jnp.where(pred(sc_ref[s, rsl, :], s), 1.0, 0.0)
            return cnt
        cnt = lax.fori_loop(0, nslab4, body, jnp.zeros((rows, LANES), F32))
        return jnp.sum(cnt, axis=1, keepdims=True)

    def gmax(c4, g):
        g = list(g)
        for j in range(4):
            g[j % ngrp] = jnp.maximum(g[j % ngrp], sc_ref[c4 * 4 + j, rsl, :])
        return tuple(g)

    g = lax.fori_loop(0, nslab4, gmax, tuple(jnp.full((rows, LANES), INT_MIN, I32) for _ in range(ngrp)))
    g_lo, g_hi = g[0], g[0]
    for x in g[1:]:
        g_lo, g_hi = jnp.minimum(g_lo, x), jnp.maximum(g_hi, x)
    lo = jnp.min(g_lo, axis=1, keepdims=True)
    hi = jnp.max(g_hi, axis=1, keepdims=True)

    width = 32 - lax.clz(hi - lo)
    nsteps = jnp.max(width.astype(F32)).astype(I32)

    def bisect(_, lohi):
        lo, hi = lohi
        d = hi - lo
        mid = lo + lax.shift_right_logical(d, 1) + (d & 1)
        ok = count(lambda k, s: k >= mid) >= kf
        live = d != 0
        return jnp.where(live & ok, mid, lo), jnp.where(live & ~ok, mid - 1, hi)

    thr, _ = lax.fori_loop(0, nsteps, bisect, (lo, hi))
    thr = jnp.maximum(thr, INT_MIN + 1)
    c_gt = count(lambda k, s: k > thr)
    c_ge = count(lambda k, s: k >= thr)
    need = kf - c_gt

    def idx_step(i, cut):
        cand = cut + jnp.left_shift(jnp.int32(1), idx_bits - 1 - i)
        c = count(lambda k, s: (k == thr) & ((s * LANES + lane) < cand))
        return jnp.where(c < need, cand, cut)

    surplus = c_ge > kf
    cut = lax.cond(jnp.max(jnp.where(surplus, 1.0, 0.0)) > 0.0,
                   lambda: lax.fori_loop(0, idx_bits, idx_step, jnp.zeros((rows, 1), I32)),
                   lambda: jnp.zeros((rows, 1), I32))
    cut = jnp.where(surplus, cut, INT_MAX)

    def fin(c4, _):
        for j in range(4):
            s = c4 * 4 + j
            k = sc_ref[s, rsl, :]
            sel = (k > thr) | ((k == thr) & ((s * LANES + lane) <= cut))
            sc_ref[s, rsl, :] = lax.bitcast_convert_type(jnp.where(sel, 0.0, NEG_INF), I32)
        return 0

    lax.fori_loop(0, nslab4, fin, 0)


def _mask_slab(sc_ref, s):
    return lax.bitcast_convert_type(sc_ref[s], F32)


def _dsa_kernel(iq_ref, iw_ref, ik_ref, dq_ref, dk_ref, dv_ref, tab_ref, o_ref, sc_ref, s_ref, mx_ref, acc_ref,
                *, tq, topk, nh, nhi, idx_bits, spc):
    qi = pl.program_id(1)
    rb = tq // LANES
    nch = (qi * rb + rb - 1) // spc + 1
    cw = spc * LANES
    sw = SCORE_TILE
    w = iw_ref[...]
    row = qi * tq + lax.broadcasted_iota(I32, (LANES, sw), 0)

    def scores(c, _):
        kt = ik_ref[c]
        col = c * sw + lax.broadcasted_iota(I32, (LANES, sw), 1)
        for r in range(rb):
            rows = slice(r * LANES, (r + 1) * LANES)
            acc = jnp.zeros((LANES, sw), F32)
            for h in range(nhi):
                acc = acc + jnp.maximum(_dot(iq_ref[h, rows, :], kt), 0.0) * w[rows, h:h + 1]
            key = jnp.where(col <= row + r * LANES, _sort_key(acc), INT_MIN)
            for j in range(sw // LANES):
                sc_ref[c * (sw // LANES) + j, rows, :] = key[:, j * LANES:(j + 1) * LANES]
        return 0

    lax.fori_loop(0, nch * (cw // sw), scores, 0)
    for r in range(rb):
        _select_topk(sc_ref, nch * (spc // 4), LANES, topk, idx_bits, row0=r * LANES)

    hs = range(HEADS_PER_STEP)
    dv = o_ref.shape[-1]

    def heads(hp, _):
        h0 = hp * HEADS_PER_STEP
        q = [dq_ref[h0 + g] for g in hs]
        mx_ref[...] = jnp.full(mx_ref.shape, M_INIT, F32)

        def p1(c, _):
            mask = [_mask_slab(sc_ref, c * spc + j) for j in range(spc)]
            tsel = [[jnp.clip(qi * rb + r - (c * spc + j), 0, 2) for r in range(rb)] for j in range(spc)]
            for g in hs:
                bias = jnp.concatenate(
                    [mask[j] + jnp.concatenate([tab_ref[tsel[j][r], h0 + g] for r in range(rb)], axis=0)
                     for j in range(spc)], axis=1)
                kt = dk_ref[c, h0 + g]
                s = _dot(q[g], kt) + bias
                s_ref[g, c] = s
                mx_ref[g] = jnp.maximum(mx_ref[g], _fold_lanes(s, jnp.maximum))
            return 0

        lax.fori_loop(0, nch, p1, 0)
        m = [jnp.max(mx_ref[g], axis=1, keepdims=True) for g in hs]
        acc_ref[...] = jnp.zeros(acc_ref.shape, F32)

        def p2(c, _):
            ks = pl.ds(pl.multiple_of(c * cw, cw), cw)
            for g in hs:
                p = jnp.exp2(s_ref[g, c] - m[g])
                acc_ref[g] += _dot(p.astype(BF16), dv_ref[h0 + g, ks, :])
            return 0

        lax.fori_loop(0, nch, p2, 0)
        for g in hs:
            acc = acc_ref[g]
            o_ref[h0 + g] = acc[:, :dv] / acc[:, dv:dv + 1]
        return 0

    lax.fori_loop(0, nh // HEADS_PER_STEP, heads, 0)


def _t5_bucket(rel, n_buckets):
    max_exact = n_buckets // 2
    n = jnp.maximum(rel, 0)
    nf = jnp.maximum(n, 1).astype(F32)
    large = max_exact + (jnp.log(nf / max_exact) / np.log(MAX_DISTANCE / max_exact) * (n_buckets - max_exact)).astype(I32)
    return jnp.where(n < max_exact, n, jnp.minimum(large, n_buckets - 1))


def _bias_lookup(rel, rel_bias):
    n_buckets = rel_bias.shape[0]
    onehot = (_t5_bucket(rel, n_buckets)[..., None] == jnp.arange(n_buckets)).astype(F32)
    return jnp.dot(onehot, rel_bias.astype(F32), precision=lax.Precision.HIGHEST)


def _dsa_prompt(iq_h, iw, ik_c, dq_h, dk_c, dv1_h, rel_bias):
    bsz, nh, seq, dh = dq_h.shape
    nhi, di = iq_h.shape[1], iq_h.shape[3]
    wv = dv1_h.shape[-1]
    n_buckets = rel_bias.shape[0]
    tq = min(2 * LANES, seq)
    spc = DSA_CHUNK // LANES
    assert seq % DSA_CHUNK == 0 and DSA_CHUNK % SCORE_TILE == 0 and LANES == MAX_DISTANCE and dh < wv
    topk = min(TOPK_MAX, seq // 4)
    i = jnp.arange(LANES)
    rel = jnp.arange(2)[:, None, None] * LANES + i[None, :, None] - i[None, None, :]
    far = rel_bias[n_buckets - 1].astype(F32)
    tab = _bias_lookup(rel, rel_bias) - far
    tab = jnp.concatenate([tab.transpose(0, 3, 1, 2) * LOG2E, jnp.zeros((1, nh, LANES, LANES), F32)], axis=0)
    nslab = seq // LANES
    kern = functools.partial(_dsa_kernel, tq=tq, topk=topk, nh=nh, nhi=nhi, idx_bits=max(1, (seq - 1).bit_length()),
                             spc=spc)
    return pl.pallas_call(
        kern,
        grid=(bsz, seq // tq),
        in_specs=[pl.BlockSpec((None, nhi, tq, di), lambda b, i: (b, 0, i, 0)),
                  pl.BlockSpec((None, tq, nhi), lambda b, i: (b, i, 0)),
                  _resident((None,) + ik_c.shape[1:], lambda b, i: (b, 0, 0, 0)),
                  pl.BlockSpec((None, nh, tq, dh), lambda b, i: (b, 0, i, 0)),
                  _resident((None,) + dk_c.shape[1:], lambda b, i: (b, 0, 0, 0, 0)),
                  _resident((None, nh, seq, wv), lambda b, i: (b, 0, 0, 0)),
                  _resident(tab.shape, lambda b, i: (0, 0, 0, 0))],
        out_specs=pl.BlockSpec((None, nh, tq, dh), lambda b, i: (b, 0, i, 0)),
        out_shape=jax.ShapeDtypeStruct((bsz, nh, seq, dh), F32),
        scratch_shapes=[pltpu.VMEM((nslab, tq, LANES), I32),
                        pltpu.VMEM((HEADS_PER_STEP, nslab // spc, tq, spc * LANES), F32),
                        pltpu.VMEM((HEADS_PER_STEP, tq, LANES), F32), pltpu.VMEM((HEADS_PER_STEP, tq, wv), F32)],
        compiler_params=_cparams(("arbitrary", "arbitrary")),
        name="dsa_prompt",
    )(iq_h, iw.reshape(bsz, seq, nhi), ik_c, dq_h, dk_c, dv1_h, tab)


def _out_kernel(x_ref, gate_ref, of_ref, fz_ref, od_ref, dz_ref, gf_ref, gd_ref, wf_ref, wd_ref, wo_ref, gp_ref, y_ref,
                *, head_major, nh, dh):
    def branch(o_ref, z_ref, w_ref):
        g = _silu(z_ref[...])
        o = jnp.concatenate([o_ref[h] for h in range(nh)], axis=1) if head_major else o_ref[...]
        return _dot((o * g).astype(BF16), w_ref[...])

    a = branch(of_ref, fz_ref, wf_ref)
    b = branch(od_ref, dz_ref, wd_ref)
    merged = _sigmoid(gf_ref[...]) * a + _sigmoid(gd_ref[...]) * b
    out = _dot(merged.astype(BF16), wo_ref[...])
    nrm = out * lax.rsqrt(jnp.mean(out * out, axis=-1, keepdims=True) + RMS_EPS) * gp_ref[...]
    y_ref[...] = x_ref[...] + gate_ref[...] * nrm


def _mix_out(x2, gate, o_fox, fz, o_dsa, dz, gf, gd, w_fox_out, w_dsa_out, w_out, g_post, *, head_major, seq, nh, dh):
    n, d = x2.shape
    wf = nh * dh
    tm = min(256, n)
    assert n % tm == 0
    row = lambda i: (i, 0)
    if head_major:
        nt = seq // tm
        o_spec = pl.BlockSpec((None, nh, tm, dh), lambda i: (i // nt, 0, i % nt, 0))
        gate_spec = pl.BlockSpec((None, 1, d), lambda i: (i // nt, 0, 0))
    else:
        o_spec = pl.BlockSpec((tm, wf), row)
        gate_spec = pl.BlockSpec((tm, d), row)
    return pl.pallas_call(
        functools.partial(_out_kernel, head_major=head_major, nh=nh, dh=dh),
        grid=(n // tm,),
        in_specs=[pl.BlockSpec((tm, d), row), gate_spec, o_spec, pl.BlockSpec((tm, wf), row),
                  o_spec, pl.BlockSpec((tm, wf), row), pl.BlockSpec((tm, d), row), pl.BlockSpec((tm, d), row),
                  _resident((wf, d), lambda i: (0, 0)), _resident((wf, d), lambda i: (0, 0)),
                  _resident((d, d), lambda i: (0, 0)), _resident((1, d), lambda i: (0, 0))],
        out_specs=pl.BlockSpec((tm, d), row),
        out_shape=jax.ShapeDtypeStruct((n, d), F32),
        compiler_params=_cparams(("arbitrary",)),
        name="mix_out_prompt" if head_major else "mix_out_sample",
    )(x2, gate, o_fox, fz, o_dsa, dz, gf, gd, w_fox_out.astype(BF16), w_dsa_out.astype(BF16), w_out.astype(BF16),
      g_post.reshape(1, d))


def _sidx_kernel(pt_ref, q_ref, w_ref, kn_ref, *refs, n_pages, t_new, past):
    pages = refs[:n_pages]
    o_ref = refs[n_pages]
    q = q_ref[...]
    w = w_ref[...]

    def slab(kt):
        r = jnp.maximum(_dot(q, kt.astype(BF16)), 0.0) * w
        return jnp.concatenate([jnp.sum(r[8 * t:8 * t + 8], axis=0, keepdims=True) for t in range(t_new)], axis=0)

    for j in range(n_pages):
        o_ref[:, j * LANES:(j + 1) * LANES] = slab(pages[j][...])
    new = slab(kn_ref[...])
    lane = lax.broadcasted_iota(I32, (t_new, LANES), 1)
    trow = lax.broadcasted_iota(I32, (t_new, LANES), 0)
    o_ref[:, past:past + LANES] = jnp.where(lane <= trow, new, NEG_INF)
    pad = o_ref.shape[-1] - past - LANES
    if pad:
        o_ref[:, past + LANES:] = jnp.full((t_new, pad), NEG_INF, F32)


def _sample_idx_scores(page_table, q32, w32, ik_new_pad, cache_idx_k, layer, lpad):
    dbsz, n_pages = page_table.shape
    t_new = q32.shape[1] // 8
    di = q32.shape[2]
    page = cache_idx_k.shape[3]
    assert page == LANES
    page_specs = [pl.BlockSpec((None, None, di, page), (lambda b, pt, j=j: (layer, pt[b, j], 0, 0))) for j in range(n_pages)]
    return pl.pallas_call(
        functools.partial(_sidx_kernel, n_pages=n_pages, t_new=t_new, past=n_pages * page),
        grid_spec=pltpu.PrefetchScalarGridSpec(
            num_scalar_prefetch=1,
            grid=(dbsz,),
            in_specs=[pl.BlockSpec((None, t_new * 8, di), lambda b, pt: (b, 0, 0)),
                      pl.BlockSpec((None, t_new * 8, LANES), lambda b, pt: (b, 0, 0)),
                      pl.BlockSpec((None, di, page), lambda b, pt: (b, 0, 0))] + page_specs,
            out_specs=pl.BlockSpec((None, t_new, lpad), lambda b, pt: (b, 0, 0)),
        ),
        out_shape=jax.ShapeDtypeStruct((dbsz, t_new, lpad), F32),
        compiler_params=_cparams(("arbitrary",)),
        name="sample_idx_scores",
    )(page_table, q32, w32, ik_new_pad, *([cache_idx_k] * n_pages))


def _ssel_kernel(s_ref, o_ref, sc_ref, *, rows, t_new, past, topk, idx_bits):
    nslab = sc_ref.shape[0]
    lane = lax.broadcasted_iota(I32, (rows, LANES), 1)
    t = lax.broadcasted_iota(I32, (rows, LANES), 0) % t_new
    for s in range(nslab):
        valid = (s * LANES + lane) <= past + t
        sc_ref[s] = jnp.where(valid, _sort_key(s_ref[:, s * LANES:(s + 1) * LANES]), INT_MIN)
    _select_topk(sc_ref, nslab // 4, rows, topk, idx_bits)
    for s in range(nslab):
        o_ref[:, s * LANES:(s + 1) * LANES] = _mask_slab(sc_ref, s)


def _sample_select(scores2, t_new, past, topk):
    n, lpad = scores2.shape
    rows = min(n, LANES)
    assert n % rows == 0 and rows % t_new == 0 and lpad % (4 * LANES) == 0
    nslab = lpad // LANES
    return pl.pallas_call(
        functools.partial(_ssel_kernel, rows=rows, t_new=t_new, past=past, topk=topk, idx_bits=max(1, (lpad - 1).bit_length())),
        grid=(n // rows,),
        in_specs=[pl.BlockSpec((rows, lpad), lambda i: (i, 0))],
        out_specs=pl.BlockSpec((rows, lpad), lambda i: (i, 0)),
        out_shape=jax.ShapeDtypeStruct((n, lpad), F32),
        scratch_shapes=[pltpu.VMEM((nslab, rows, LANES), I32)],
        compiler_params=_cparams(("arbitrary",)),
        name="sample_select",
    )(scores2)


def _wt_kernel(lp_ref, o_ref):
    pb = lp_ref.shape[0]
    lp = lp_ref[...].reshape(pb * 8, LANES)
    r = lax.broadcasted_iota(I32, (LANES, 2 * LANES), 0)
    c = lax.broadcasted_iota(I32, (LANES, 2 * LANES), 1)
    u = jnp.where((c >= LANES) | (r > c), 1.0, 0.0).astype(BF16)
    hi, mid, lo = _split3(lp)
    res = _dot(lo, u) + _dot(mid, u) + _dot(hi, u)
    o_ref[:, 0:8, :] = res[:, :LANES].reshape(pb, 8, LANES)
    o_ref[:, 8:16, :] = res[:, LANES:].reshape(pb, 8, LANES)


def _page_suffix(logf_t):
    n_phys = logf_t.shape[0]
    pb = 256
    while n_phys % pb:
        pb //= 2
    return pl.pallas_call(
        _wt_kernel,
        grid=(n_phys // pb,),
        in_specs=[pl.BlockSpec((pb, 8, LANES), lambda i: (i, 0, 0))],
        out_specs=pl.BlockSpec((pb, 16, LANES), lambda i: (i, 0, 0)),
        out_shape=jax.ShapeDtypeStruct((n_phys, 16, LANES), F32),
        compiler_params=_cparams(("arbitrary",)),
        name="page_suffix",
    )(logf_t)


def _sattn_kernel(pt_ref, qf_ref, qd_ref, knf_ref, vnf_ref, knd_ref, vnd_ref, cn_ref, bnf_ref, selb_ref, seln_ref,
                  t5_ref, t5n_ref, *refs, ppc, nc, t_new, nh, dh):
    kf_p, vf_p, kd_p, vd_p, wt_p = (refs[i * ppc:(i + 1) * ppc] for i in range(5))
    of_ref, od_ref = refs[5 * ppc:5 * ppc + 2]
    qbf, qbd, mf, lf, af, md, ld, ad, car = refs[5 * ppc + 2:]
    c = pl.program_id(1)
    rows = t_new * 8
    wf = nh * dh
    headmask = (lax.broadcasted_iota(I32, (8, wf), 1) // dh) == lax.broadcasted_iota(I32, (8, wf), 0)

    def expand_rows(x):
        return jnp.concatenate([jnp.broadcast_to(x[t:t + 1], (8, x.shape[1])) for t in range(t_new)], axis=0)

    def step(state, q, ks, bias, vs):
        m_ref, l_ref, a_ref = state
        s = jnp.concatenate([_dot(q, k.astype(BF16)) for k in ks], axis=1) + bias
        m_old = m_ref[...]
        m_new = jnp.maximum(m_old, jnp.max(s, axis=1, keepdims=True))
        alpha = jnp.exp(m_old - m_new)
        p = jnp.exp(s - m_new)
        pb = p.astype(BF16)
        pv = None
        for j, v in enumerate(vs):
            t = _dot_nt(pb[:, j * LANES:(j + 1) * LANES], v.astype(BF16))
            pv = t if pv is None else pv + t
        m_ref[...] = m_new
        l_ref[...] = alpha * l_ref[...] + jnp.sum(p, axis=1, keepdims=True)
        a_ref[...] = alpha * a_ref[...] + pv

    @pl.when(c == 0)
    def _():
        hm = jnp.where(headmask, 1.0, 0.0)
        qbf[...] = (expand_rows(qf_ref[...].astype(F32)) * jnp.concatenate([hm] * t_new, axis=0)).astype(BF16)
        qbd[...] = (expand_rows(qd_ref[...].astype(F32)) * jnp.concatenate([hm] * t_new, axis=0)).astype(BF16)
        for m_ref, l_ref, a_ref in ((mf, lf, af), (md, ld, ad)):
            m_ref[...] = jnp.full(m_ref.shape, M_INIT, F32)
            l_ref[...] = jnp.zeros(l_ref.shape, F32)
            a_ref[...] = jnp.zeros(a_ref.shape, F32)
        car[...] = jnp.zeros(car.shape, F32)
        step((mf, lf, af), qbf[...], [knf_ref[...]], bnf_ref[...], [vnf_ref[...]])
        step((md, ld, ad), qbd[...], [knd_ref[...]], expand_rows(seln_ref[...]) + t5n_ref[...], [vnd_ref[...]])

    carry = car[...]
    cn = cn_ref[...]
    bias = [None] * ppc
    for j in reversed(range(ppc)):
        wt = wt_p[j][...]
        suf = wt[0:8] + carry
        bias[j] = jnp.concatenate([suf] * t_new, axis=0) + cn
        carry = carry + wt[8:16]
    car[...] = carry
    step((mf, lf, af), qbf[...], [r[...] for r in kf_p], jnp.concatenate(bias, axis=1), [r[...] for r in vf_p])
    step((md, ld, ad), qbd[...], [r[...] for r in kd_p], expand_rows(selb_ref[...]) + t5_ref[nc - 1 - c],
         [r[...] for r in vd_p])

    @pl.when(c == nc - 1)
    def _():
        hm = jnp.concatenate([jnp.where(headmask, 1.0, 0.0)] * t_new, axis=0)
        for (l_ref, a_ref), o_ref in (((lf, af), of_ref), ((ld, ad), od_ref)):
            o = (a_ref[...] / l_ref[...]) * hm
            o_ref[...] = jnp.concatenate([jnp.sum(o[8 * t:8 * t + 8], axis=0, keepdims=True) for t in range(t_new)], axis=0)


def _pages_per_step(n_pages):
    ppc = PAGES_PER_STEP
    while n_pages % ppc:
        ppc //= 2
    return ppc


def _sample_attention(page_table, layer, qf, qd, knf, vnf, knd, vnd, cn32, bnf, selb, seln, t5c, t5n,
                      cache_fox_k, cache_fox_v, cache_dsa_k, cache_dsa_v, wt, *, nh, dh):
    dbsz, n_pages = page_table.shape
    t_new = qf.shape[1]
    wf = nh * dh
    page = cache_fox_k.shape[3]
    ppc = _pages_per_step(n_pages)
    nc = n_pages // ppc
    rows = t_new * 8
    cwid = ppc * page
    assert page == LANES and nh == 8

    def pg(j):
        return lambda b, c, pt: (layer, pt[b, (nc - 1 - c) * ppc + j], 0, 0)

    def wtm(j):
        return lambda b, c, pt: (pt[b, (nc - 1 - c) * ppc + j], 0, 0)

    kv_specs = [pl.BlockSpec((None, None, wf, page), pg(j)) for j in range(ppc)]
    wt_specs = [pl.BlockSpec((None, 16, LANES), wtm(j)) for j in range(ppc)]
    per_b3 = lambda shp: pl.BlockSpec((None,) + shp, lambda b, c, pt: (b, 0, 0))
    caches = lambda a: [a] * ppc
    return pl.pallas_call(
        functools.partial(_sattn_kernel, ppc=ppc, nc=nc, t_new=t_new, nh=nh, dh=dh),
        grid_spec=pltpu.PrefetchScalarGridSpec(
            num_scalar_prefetch=1,
            grid=(dbsz, nc),
            in_specs=[per_b3((t_new, wf)), per_b3((t_new, wf)),
                      per_b3((wf, page)), per_b3((wf, page)), per_b3((wf, page)), per_b3((wf, page)),
                      per_b3((rows, LANES)), per_b3((rows, LANES)),
                      pl.BlockSpec((None, None, t_new, cwid), lambda b, c, pt: (b, nc - 1 - c, 0, 0)),
                      per_b3((t_new, LANES)),
                      _resident(t5c.shape, lambda b, c, pt: (0, 0, 0)),
                      _resident(t5n.shape, lambda b, c, pt: (0, 0))]
                     + kv_specs * 4 + wt_specs,
            out_specs=[per_b3((t_new, wf)), per_b3((t_new, wf))],
            scratch_shapes=[pltpu.VMEM((rows, wf), BF16), pltpu.VMEM((rows, wf), BF16),
                            pltpu.VMEM((rows, 1), F32), pltpu.VMEM((rows, 1), F32), pltpu.VMEM((rows, wf), F32),
                            pltpu.VMEM((rows, 1), F32), pltpu.VMEM((rows, 1), F32), pltpu.VMEM((rows, wf), F32),
                            pltpu.VMEM((8, LANES), F32)],
        ),
        out_shape=[jax.ShapeDtypeStruct((dbsz, t_new, wf), F32)] * 2,
        compiler_params=_cparams(("arbitrary", "arbitrary")),
        name="sample_attention",
    )(page_table, qf, qd, knf, vnf, knd, vnd, cn32, bnf, selb, seln, t5c, t5n,
      *caches(cache_fox_k), *caches(cache_fox_v), *caches(cache_dsa_k), *caches(cache_dsa_v), *([wt] * ppc))


def kernel(x_prompt, x_sample, cache_fox_k, cache_fox_v, cache_fox_logf, cache_dsa_k, cache_dsa_v, cache_idx_k,
           page_table, c_prompt, c_sample, g_pre, w_mod, b_mod, w_in, b_forget, rel_bias,
           w_fox_out, w_dsa_out, w_out, g_post):
    bsz, seq, d = x_prompt.shape
    dbsz, t_new, _ = x_sample.shape
    depth, n_phys, page, nh, dh = cache_fox_k.shape
    di = cache_idx_k.shape[-1]
    kv_t = lambda a: a.transpose(0, 1, 3, 4, 2).reshape(depth, n_phys, nh * dh, page)
    cache_fox_k, cache_fox_v, cache_dsa_k, cache_dsa_v = map(kv_t, (cache_fox_k, cache_fox_v, cache_dsa_k, cache_dsa_v))
    cache_idx_k = cache_idx_k.transpose(0, 1, 3, 2)
    wf = nh * dh
    nhi = w_in.shape[-1] - (8 * wf + nh + di + 2 * d)
    nhi = nhi // (di + 1)
    n_pages = page_table.shape[1]
    past = n_pages * page
    n_buckets = rel_bias.shape[0]
    attn_scale = dh ** -0.5
    idx_w_scale = (nhi * di) ** -0.5
    geom = dict(nh=nh, dh=dh, nhi=nhi, di=di, attn_scale=attn_scale, idx_w_scale=idx_w_scale)
    assert nh == 8 and nhi == 8 and page == LANES

    new_p = [[] for _ in range(6)]
    new_s = [[] for _ in range(6)]
    h_p, h_s = x_prompt, x_sample
    n_c = bsz + dbsz
    n_c_pad = -(-n_c // 8) * 8
    c_all = jnp.pad(jnp.concatenate([c_prompt, c_sample], axis=0), ((0, n_c_pad - n_c), (0, 0)))
    topk_s = min(TOPK_MAX, (past + t_new) // 4)
    lpad = past + 4 * LANES

    for l in range(depth):
        mod = _mod(c_all, w_mod[l], b_mod[l])
        shift, scale, gate = mod[:, :d], mod[:, d:2 * d], mod[:, 2 * d:]
        rep = lambda a: jnp.repeat(a[bsz:n_c], t_new, axis=0)

        x2 = h_p.reshape(bsz * seq, d)
        (fk_t, fv_t, lf_t, dk_t, dv_t, ik_t, fz, dz, gf, gd, iw, qa, ka, fv_h, dq_h, dk_c, dv_h, iq_h, ik_c) = _proj(
            x2, scale[:bsz, None, :], shift[:bsz, None, :], g_pre[l], w_in[l], b_forget[l],
            prompt=True, bsz=bsz, seq=seq, **geom)
        o_fox = _fox_prompt(qa, ka, fv_h, dh)
        dk_c = dk_c.reshape(bsz, seq // DSA_CHUNK, nh, dh, DSA_CHUNK)
        o_dsa = _dsa_prompt(iq_h, iw, ik_c, dq_h, dk_c, dv_h, rel_bias)
        y2 = _mix_out(x2, gate[:bsz, None, :], o_fox, fz, o_dsa, dz, gf, gd, w_fox_out[l], w_dsa_out[l], w_out[l],
                      g_post[l], head_major=True, seq=seq, nh=nh, dh=dh)
        h_p = y2.reshape(bsz, seq, d)
        for lst, a, shp in zip(new_p, (fk_t, fv_t, lf_t, dk_t, dv_t, ik_t),
                               ((nh, dh), (nh, dh), (nh,), (nh, dh), (nh, dh), (di,))):
            a = a.reshape((bsz,) + shp + (seq,))
            lst.append(jnp.moveaxis(a, -1, 1))

        xs2 = h_s.reshape(dbsz * t_new, d)
        (sfk, sfv, slf, sdk, sdv, sik, sfz, sdz, sgf, sgd, siw, sfq, sdq, siq, scn) = _proj(
            xs2, rep(scale), rep(shift), g_pre[l], w_in[l], b_forget[l],
            prompt=False, bsz=dbsz, seq=t_new, **geom)
        q32 = siq.reshape(dbsz, t_new * nhi, di)
        w32 = jnp.broadcast_to(siw.reshape(dbsz, t_new * nhi, 1), (dbsz, t_new * nhi, LANES))
        pad_new = lambda a: jnp.pad(a.reshape(dbsz, t_new, -1).transpose(0, 2, 1), ((0, 0), (0, 0), (0, page - t_new)))
        scores = _sample_idx_scores(page_table, q32, w32, pad_new(sik), cache_idx_k, l, lpad)
        sel = _sample_select(scores.reshape(dbsz * t_new, lpad), t_new, past, topk_s).reshape(dbsz, t_new, lpad)
        tpos = jnp.arange(t_new)
        rel = past + tpos[:, None] - jnp.arange(past + page)[None, :]
        t5 = _bias_lookup(rel, rel_bias)
        t5 = t5.transpose(0, 2, 1).reshape(t_new * nh, past + page)
        ppc = _pages_per_step(n_pages)
        nc = n_pages // ppc
        t5c =t5[:, :past].reshape(t_new * nh, nc, ppc * page).transpose(1, 0, 2)
        t5n = t5[:, past:]
        selb = sel[:, :, :past].reshape(dbsz, t_new, nc, ppc * page).transpose(0, 2, 1, 3)
        seln = sel[:, :, past:past + page]
        cn = scn[:, :nh].reshape(dbsz, t_new, nh)
        cn32 = jnp.broadcast_to(cn.reshape(dbsz, t_new * nh, 1), (dbsz, t_new * nh, LANES))
        dcn = cn[:, :, None, :] - cn[:, None, :, :]
        dcn = jnp.where((tpos[None, :] <= tpos[:, None])[None, :, :, None], dcn, NEG_INF)
        bnf = jnp.pad(dcn.transpose(0, 1, 3, 2).reshape(dbsz, t_new * nh, t_new), ((0, 0), (0, 0), (0, LANES - t_new)),
                      constant_values=NEG_INF)
        wt = _page_suffix(cache_fox_logf[l].transpose(0, 2, 1))
        so_fox, so_dsa = _sample_attention(
            page_table, l, sfq.reshape(dbsz, t_new, wf), sdq.reshape(dbsz, t_new, wf),
            pad_new(sfk), pad_new(sfv), pad_new(sdk), pad_new(sdv), cn32, bnf, selb, seln, t5c, t5n,
            cache_fox_k, cache_fox_v, cache_dsa_k, cache_dsa_v, wt, nh=nh, dh=dh)
        ys2 = _mix_out(xs2, rep(gate), so_fox.reshape(dbsz * t_new, wf), sfz, so_dsa.reshape(dbsz * t_new, wf), sdz,
                       sgf, sgd, w_fox_out[l], w_dsa_out[l], w_out[l], g_post[l], head_major=False, seq=t_new,
                       nh=nh, dh=dh)
        h_s = ys2.reshape(dbsz, t_new, d)
        for lst, a, shp in zip(new_s, (sfk, sfv, slf, sdk, sdv, sik),
                               ((nh, dh), (nh, dh), (nh,), (nh, dh), (nh, dh), (di,))):
            lst.append(a.reshape((dbsz, t_new) + shp))

    return (h_p, h_s) + tuple(jnp.stack(v) for v in new_p) + tuple(jnp.stack(v) for v in new_s)
```
